```python
import jax, jax.numpy as jnp
from jax import lax
import numpy as np

D_MODEL = 2048
BATCH = 8
SEQ = 4096
DEPTH = 1

N_META = 16
POOL_WINDOWS = (2, 4, 8, 16)
N_POOL_GROUPS = 4
POOL_WIDTH = D_MODEL // 2
POOL_GROUP_DIM = POOL_WIDTH // N_POOL_GROUPS
N_HEADS = 16
HEAD_DIM = (D_MODEL // 2) // N_HEADS
ATTN_WIDTH = N_HEADS * HEAD_DIM
ROT_DIM = HEAD_DIM // 4
ROPE_THETA = 500000.0
N_IDX_HEADS = 16
IDX_DIM = 64
IDX_ROT_DIM = IDX_DIM // 4
TOPK_KEYS_MAX = 256
Q_BLOCK = 128
N_EXPERTS = 32
TOP_K = 4
D_FF = D_MODEL
SWIGLU_LIMIT = 7.0
SWIGLU_ALPHA = 1.702
MOE_BLOCK = 512
EPS = 1e-5
IN_WIDTHS = (POOL_WIDTH, ATTN_WIDTH, ATTN_WIDTH, ATTN_WIDTH, N_IDX_HEADS * IDX_DIM, IDX_DIM, N_IDX_HEADS, D_MODEL, D_MODEL)
D_IN = sum(IN_WIDTHS)

kernel_name = "hybrid_pool_dsa_gated_moe_block"


def rms_norm(x, g):
    xf = x.astype(jnp.float32)
    y = xf * lax.rsqrt(jnp.mean(xf * xf, axis=-1, keepdims=True) + EPS)
    return (y * g.astype(jnp.float32)).astype(x.dtype)


def partial_rope(t, pos, rot_dim):
    half = rot_dim // 2
    freqs = ROPE_THETA ** (-jnp.arange(0, rot_dim, 2, dtype=jnp.float32) / rot_dim)
    ang = pos.astype(jnp.float32)[:, None] * freqs[None, :]
    cos = jnp.cos(ang)[None, :, None, :]
    sin = jnp.sin(ang)[None, :, None, :]
    x1 = t[..., :half].astype(jnp.float32)
    x2 = t[..., half:rot_dim].astype(jnp.float32)
    rot = jnp.concatenate([x1 * cos - x2 * sin, x2 * cos + x1 * sin], axis=-1).astype(t.dtype)
    return jnp.concatenate([rot, t[..., rot_dim:]], axis=-1)


def split_columns(proj):
    offs = []
    acc = 0
    for w in IN_WIDTHS[:-1]:
        acc += w
        offs.append(acc)
    return jnp.split(proj, offs, axis=-1)


def multiscale_pool(u, pool_w, pool_scale):
    B, L, _ = u.shape
    ug = u.reshape(B, L, N_POOL_GROUPS, POOL_GROUP_DIM)
    c0 = jnp.concatenate([jnp.zeros((B, 1, N_POOL_GROUPS, POOL_GROUP_DIM), jnp.float32),
                          jnp.cumsum(ug.astype(jnp.float32), axis=1)], axis=1)
    t = jnp.arange(L)
    outs = []
    for g, w in enumerate(POOL_WINDOWS):
        c = c0[:, :, g]
        lower = jnp.concatenate([jnp.zeros((B, w - 1, POOL_GROUP_DIM), jnp.float32), c[:, :L - w + 1]], axis=1)
        count = jnp.minimum(t + 1, w).astype(jnp.float32)[None, :, None]
        outs.append((c[:, 1:] - lower) / count)
    pooled = jnp.stack(outs, axis=2).astype(u.dtype) - ug
    mixed = jnp.einsum('blgc,gcd->blgd', pooled, pool_w)
    return mixed.reshape(B, L, POOL_WIDTH) * pool_scale


def indexed_sparse_attention(q, k, v, iq, ik, iw):
    B, L, _ = q.shape
    pos = jnp.arange(L)
    q = partial_rope(q.reshape(B, L, N_HEADS, HEAD_DIM), pos, ROT_DIM)
    k = partial_rope(k.reshape(B, L, N_HEADS, HEAD_DIM), pos, ROT_DIM)
    v = v.reshape(B, L, N_HEADS, HEAD_DIM)
    iq = partial_rope(iq.reshape(B, L, N_IDX_HEADS, IDX_DIM), pos, IDX_ROT_DIM)
    ik = partial_rope(ik[:, :, None, :], pos, IDX_ROT_DIM)[:, :, 0]
    iw = iw * (N_IDX_HEADS ** -0.5)
    n_sel = min(TOPK_KEYS_MAX, L // 4)
    n_blocks = -(-L // Q_BLOCK)
    l_pad = n_blocks * Q_BLOCK
    q_p = jnp.pad(q, ((0, 0), (0, l_pad - L), (0, 0), (0, 0)))
    iq_p = jnp.pad(iq, ((0, 0), (0, l_pad - L), (0, 0), (0, 0)))
    iw_p = jnp.pad(iw, ((0, 0), (0, l_pad - L), (0, 0)))
    ik_f = ik.astype(jnp.float32)
    key_pos = jnp.arange(L)

    def attend_block(b):
        start = b * Q_BLOCK
        qb = lax.dynamic_slice_in_dim(q_p, start, Q_BLOCK, axis=1)
        iqb = lax.dynamic_slice_in_dim(iq_p, start, Q_BLOCK, axis=1)
        iwb = lax.dynamic_slice_in_dim(iw_p, start, Q_BLOCK, axis=1)
        q_pos = start + jnp.arange(Q_BLOCK)
        causal = key_pos[None, :] <= q_pos[:, None]
        idx_logits = jnp.einsum('bqhd,bsd->bqhs', iqb.astype(jnp.float32), ik_f) * (IDX_DIM ** -0.5)
        score = jnp.einsum('bqhs,bqh->bqs', jax.nn.relu(idx_logits), iwb.astype(jnp.float32))
        score = jnp.where(causal[None], score, -jnp.inf)
        _, sel = lax.top_k(score, n_sel)
        valid = sel <= q_pos[None, :, None]
        k_sel = jax.vmap(lambda kk, ii: kk[ii])(k, sel)
        v_sel = jax.vmap(lambda vv, ii: vv[ii])(v, sel)
        s = jnp.einsum('bqhd,bqkhd->bqhk', qb, k_sel).astype(jnp.float32) * (HEAD_DIM ** -0.5)
        s = jnp.where(valid[:, :, None, :], s, -jnp.inf)
        p = jax.nn.softmax(s, axis=-1).astype(v.dtype)
        return jnp.einsum('bqhk,bqkhd->bqhd', p, v_sel)

    o = lax.map(attend_block, jnp.arange(n_blocks))
    o = jnp.moveaxis(o, 0, 1).reshape(B, l_pad, ATTN_WIDTH)
    return o[:, :L]


def gated_hybrid_mixer(xn, w_in, pool_w, pool_scale, w_branch_pool, w_branch_attn, w_out):
    proj = jnp.einsum('bld,de->ble', xn, w_in)
    u, q, k, v, iq, ik, iw, g_pool, g_attn = split_columns(proj)
    y_pool = jnp.einsum('blc,cd->bld', multiscale_pool(u, pool_w, pool_scale), w_branch_pool)
    y_attn = jnp.einsum('blc,cd->bld', indexed_sparse_attention(q, k, v, iq, ik, iw), w_branch_attn)
    merged = jax.nn.sigmoid(g_pool) * y_pool + jax.nn.sigmoid(g_attn) * y_attn
    return jnp.einsum('bld,de->ble', merged, w_out)


def moe_ffn(xn, router_w, router_b, w_gate, b_gate, w_up, b_up, w_down, b_down):
    B, L, D = xn.shape
    T = B * L
    xt = xn.reshape(T, D)
    logits = jnp.einsum('td,de->te', xt.astype(jnp.float32), router_w.astype(jnp.float32)) + router_b.astype(jnp.float32)
    top_val, top_idx = lax.top_k(logits, TOP_K)
    gates = jax.nn.softmax(top_val, axis=-1)
    n_assign = T * TOP_K
    flat_e = top_idx.reshape(n_assign).astype(jnp.int32)
    flat_tok = jnp.repeat(jnp.arange(T, dtype=jnp.int32), TOP_K)
    flat_gate = gates.reshape(n_assign)
    order = jnp.argsort(flat_e, stable=True)
    sorted_e = flat_e[order]
    counts = jnp.bincount(flat_e, length=N_EXPERTS).astype(jnp.int32)
    padded = ((counts + MOE_BLOCK - 1) // MOE_BLOCK) * MOE_BLOCK
    start = jnp.cumsum(counts) - counts
    pad_end = jnp.cumsum(padded)
    pad_start = pad_end - padded
    rank = jnp.arange(n_assign, dtype=jnp.int32) - start[sorted_e]
    dest = pad_start[sorted_e] + rank
    n_blocks = -(-(n_assign + N_EXPERTS * (MOE_BLOCK - 1)) // MOE_BLOCK)
    n_rows = n_blocks * MOE_BLOCK
    row_tok = jnp.full((n_rows,), T, jnp.int32).at[dest].set(flat_tok[order])
    row_gate = jnp.zeros((n_rows,), jnp.float32).at[dest].set(flat_gate[order])
    block_expert = jnp.minimum(jnp.searchsorted(pad_end, jnp.arange(n_blocks) * MOE_BLOCK, side='right'), N_EXPERTS - 1)
    x_pad = jnp.concatenate([xt, jnp.zeros((1, D), xt.dtype)], axis=0)

    def expert_block(b):
        tok = lax.dynamic_slice_in_dim(row_tok, b * MOE_BLOCK, MOE_BLOCK)
        gw = lax.dynamic_slice_in_dim(row_gate, b * MOE_BLOCK, MOE_BLOCK)
        e = block_expert[b]
        xb = x_pad[tok]
        gt = xb @ w_gate[e] + b_gate[e]
        up = xb @ w_up[e] + b_up[e]
        gt = jnp.minimum(gt, SWIGLU_LIMIT)
        up = jnp.clip(up, -SWIGLU_LIMIT, SWIGLU_LIMIT)
        hidden = (up + 1) * (gt * jax.nn.sigmoid(SWIGLU_ALPHA * gt))
        y = hidden @ w_down[e] + b_down[e]
        return y * gw[:, None].astype(y.dtype)

    rows = lax.map(expert_block, jnp.arange(n_blocks)).reshape(n_rows, D)
    out = jax.ops.segment_sum(rows, row_tok, num_segments=T + 1)[:T]
    return out.reshape(B, L, D)


def setup_inputs(seed: int = 0) -> dict:
    key = jax.random.key(seed)
    ks = jax.random.split(key, 20)
    f32 = jnp.float32
    nrm = lambda k, shape, scale: jax.random.normal(k, shape, f32) * scale
    return {
        "x": nrm(ks[0], (BATCH, SEQ, D_MODEL), 1.0),
        "meta_tokens": nrm(ks[1], (N_META, D_MODEL), 1.0),
        "norm_mix_g": 1.0 + nrm(ks[2], (DEPTH, D_MODEL), 0.02),
        "w_in": nrm(ks[3], (DEPTH, D_MODEL, D_IN), D_MODEL ** -0.5),
        "pool_w": nrm(ks[4], (DEPTH, N_POOL_GROUPS, POOL_GROUP_DIM, POOL_GROUP_DIM), POOL_GROUP_DIM ** -0.5),
        "pool_scale": 1.0 + nrm(ks[5], (DEPTH, POOL_WIDTH), 0.02),
        "w_branch_pool": nrm(ks[6], (DEPTH, POOL_WIDTH, D_MODEL), POOL_WIDTH ** -0.5),
        "w_branch_attn": nrm(ks[7], (DEPTH, ATTN_WIDTH, D_MODEL), ATTN_WIDTH ** -0.5),
        "w_out": nrm(ks[8], (DEPTH, D_MODEL, D_MODEL), D_MODEL ** -0.5),
        "norm_moe_g": 1.0 + nrm(ks[9], (DEPTH, D_MODEL), 0.02),
        "router_w": nrm(ks[10], (DEPTH, D_MODEL, N_EXPERTS), D_MODEL ** -0.5),
        "router_b": nrm(ks[11], (DEPTH, N_EXPERTS), 0.01),
        "exp_w_gate": nrm(ks[12], (DEPTH, N_EXPERTS, D_MODEL, D_FF), D_MODEL ** -0.5),
        "exp_b_gate": nrm(ks[13], (DEPTH, N_EXPERTS, D_FF), 0.02),
        "exp_w_up": nrm(ks[14], (DEPTH, N_EXPERTS, D_MODEL, D_FF), D_MODEL ** -0.5),
        "exp_b_up": nrm(ks[15], (DEPTH, N_EXPERTS, D_FF), 0.02),
        "exp_w_down": nrm(ks[16], (DEPTH, N_EXPERTS, D_FF, D_MODEL), D_FF ** -0.5),
        "exp_b_down": nrm(ks[17], (DEPTH, N_EXPERTS, D_MODEL), 0.02),
        "norm_final_g": 1.0 + nrm(ks[18], (D_MODEL,), 0.02),
    }


def reference(x, meta_tokens, norm_mix_g, w_in, pool_w, pool_scale, w_branch_pool, w_branch_attn, w_out,
              norm_moe_g, router_w, router_b, exp_w_gate, exp_b_gate, exp_w_up, exp_b_up, exp_w_down,
              exp_b_down, norm_final_g):
    B = x.shape[0]
    meta = jnp.broadcast_to(meta_tokens[None].astype(x.dtype), (B, N_META, D_MODEL))
    h = jnp.concatenate([meta, x], axis=1)
    for i in range(DEPTH):
        h = h + gated_hybrid_mixer(rms_norm(h, norm_mix_g[i]), w_in[i], pool_w[i], pool_scale[i],
                                   w_branch_pool[i], w_branch_attn[i], w_out[i])
        h = h + moe_ffn(rms_norm(h, norm_moe_g[i]), router_w[i], router_b[i], exp_w_gate[i], exp_b_gate[i],
                        exp_w_up[i], exp_b_up[i], exp_w_down[i], exp_b_down[i])
    h = rms_norm(h, norm_final_g)
    return h[:, N_META:]
```

```python
import functools

import jax
import jax.numpy as jnp
from jax import lax
from jax.experimental import pallas as pl
from jax.experimental.pallas import tpu as pltpu

f32 = jnp.float32
bf16 = jnp.bfloat16
i32 = jnp.int32

D_MODEL = 2048
N_META = 16
POOL_WINDOWS = (2, 4, 8, 16)
N_POOL_GROUPS = 4
POOL_WIDTH = D_MODEL // 2
POOL_GROUP_DIM = POOL_WIDTH // N_POOL_GROUPS
N_HEADS = 16
HEAD_DIM = 64
ATTN_WIDTH = N_HEADS * HEAD_DIM
ROT_DIM = HEAD_DIM // 4
ROT_HALF = ROT_DIM // 2
ROPE_THETA = 500000.0
N_IDX_HEADS = 16
IDX_DIM = 64
TOPK_KEYS_MAX = 256
N_EXPERTS = 32
TOP_K = 4
D_FF = D_MODEL
SWIGLU_LIMIT = 7.0
SWIGLU_ALPHA = 1.702
EPS = 1e-5

LANES = 128
VMEM_LIMIT = 56 * 1024 * 1024

NORM_TM = 512
PROJ_TM = 512
ATTN_QB = 256
POOL_TM = 512
POST_TM = 256
MOE_BM = 1024
MOE_TF = 256
DISPATCH_TM = 512
COMBINE_TM = 256

NEG_BIG = -1e30
INT_MIN = -(2 ** 31)


def _cparams(sem):
    return pltpu.CompilerParams(dimension_semantics=sem, vmem_limit_bytes=VMEM_LIMIT)


def _rmsnorm_kernel(x_ref, g_ref, o_ref):
    x = x_ref[...]
    ms = jnp.mean(x * x, axis=-1, keepdims=True)
    o_ref[...] = (x * lax.rsqrt(ms + EPS) * g_ref[...]).astype(o_ref.dtype)


def _rmsnorm(x2d, g, tm):
    n = x2d.shape[0]
    return pl.pallas_call(
        _rmsnorm_kernel,
        grid=(n // tm,),
        in_specs=[pl.BlockSpec((tm, D_MODEL), lambda i: (i, 0)),
                  pl.BlockSpec((1, D_MODEL), lambda i: (0, 0))],
        out_specs=pl.BlockSpec((tm, D_MODEL), lambda i: (i, 0)),
        out_shape=jax.ShapeDtypeStruct((n, D_MODEL), bf16),
        compiler_params=_cparams(("parallel",)),
        name="rmsnorm",
    )(x2d, g.reshape(1, D_MODEL))


def _proj_kernel(xn_ref, w_ref, o_ref, *, act):
    y = jnp.dot(xn_ref[...], w_ref[...], preferred_element_type=f32)
    if act == "sigmoid":
        y = jax.nn.sigmoid(y)
    o_ref[...] = y.astype(o_ref.dtype)


def _proj(xn, w, tm, out_dtype, act=None, tn=1024):
    n, width = xn.shape[0], w.shape[1]
    tn = min(tn, width)
    return pl.pallas_call(
        functools.partial(_proj_kernel, act=act),
        grid=(width // tn, n // tm),
        in_specs=[pl.BlockSpec((tm, D_MODEL), lambda j, i: (i, 0)),
                  pl.BlockSpec((D_MODEL, tn), lambda j, i: (0, j))],
        out_specs=pl.BlockSpec((tm, tn), lambda j, i: (i, j)),
        out_shape=jax.ShapeDtypeStruct((n, width), out_dtype),
        compiler_params=_cparams(("parallel", "parallel")),
        name="proj_" + (act or "plain"),
    )(xn, w)


def _proj_rope_kernel(xn_ref, w_ref, c_ref, s1_ref, s2_ref, o_ref, *, scale):
    y = jnp.dot(xn_ref[...], w_ref[...], preferred_element_type=f32)
    c, s1, s2 = c_ref[...], s1_ref[...], s2_ref[...]
    for j in range(y.shape[1] // LANES):
        yt = y[:, j * LANES:(j + 1) * LANES]
        r = yt * c + pltpu.roll(yt, ROT_HALF, 1) * s1 + pltpu.roll(yt, LANES - ROT_HALF, 1) * s2
        o_ref[:, j * LANES:(j + 1) * LANES] = (r * scale).astype(o_ref.dtype)


def _proj_rope(xn, w, tables, tm, pos_blocks, scale):
    n, width = xn.shape[0], w.shape[1]
    tspec = pl.BlockSpec((tm, LANES), lambda i: (i % pos_blocks, 0))
    return pl.pallas_call(
        functools.partial(_proj_rope_kernel, scale=scale),
        grid=(n // tm,),
        in_specs=[pl.BlockSpec((tm, D_MODEL), lambda i: (i, 0)),
                  pl.BlockSpec((D_MODEL, width), lambda i: (0, 0)),
                  tspec, tspec, tspec],
        out_specs=pl.BlockSpec((tm, width), lambda i: (i, 0)),
        out_shape=jax.ShapeDtypeStruct((n, width), bf16),
        compiler_params=_cparams(("parallel",)),
        name="proj_rope",
    )(xn, w, *tables)


def _projT_rope_kernel(wT_ref, xn_ref, cos_ref, sin_ref, o_ref, *, n_heads):
    y = lax.dot_general(wT_ref[...], xn_ref[...], (((1,), (1,)), ((), ())),
                        preferred_element_type=f32)
    c, s = cos_ref[...], sin_ref[...]
    for h in range(n_heads):
        blk = y[h * HEAD_DIM:(h + 1) * HEAD_DIM]
        x1, x2 = blk[0:ROT_HALF], blk[ROT_HALF:ROT_DIM]
        out = jnp.concatenate([x1 * c - x2 * s, x2 * c + x1 * s, blk[ROT_DIM:]], axis=0)
        o_ref[h * HEAD_DIM:(h + 1) * HEAD_DIM, :] = out.astype(o_ref.dtype)


def _projT_rope(xn, wT, cosT, sinT, tm, pos_blocks, n_heads):
    n, rows = xn.shape[0], wT.shape[0]
    tspec = pl.BlockSpec((ROT_HALF, tm), lambda i: (0, i % pos_blocks))
    return pl.pallas_call(
        functools.partial(_projT_rope_kernel, n_heads=n_heads),
        grid=(n // tm,),
        in_specs=[pl.BlockSpec((rows, D_MODEL), lambda i: (0, 0)),
                  pl.BlockSpec((tm, D_MODEL), lambda i: (i, 0)),
                  tspec, tspec],
        out_specs=pl.BlockSpec((None, rows, tm), lambda i: (i, 0, 0)),
        out_shape=jax.ShapeDtypeStruct((n // tm, rows, tm), bf16),
        compiler_params=_cparams(("parallel",)),
        name="projT_rope",
    )(wT, xn, cosT, sinT)


def _pool_kernel(u_ref, prev_ref, meta_ref, pw_ref, ps_ref, o_ref, ext_ref, *, tm):
    i = pl.program_id(1)
    halo = jnp.where(i == 0, meta_ref[...], prev_ref[...])
    ext_ref[0:N_META, :] = halo
    ext_ref[N_META:, :] = u_ref[...]
    pos = i * tm + lax.broadcasted_iota(i32, (tm, 1), 0) + N_META
    for g, w in enumerate(POOL_WINDOWS):
        cols = slice(g * POOL_GROUP_DIM, (g + 1) * POOL_GROUP_DIM)
        acc = ext_ref[N_META:, cols]
        for j in range(1, w):
            acc = acc + ext_ref[pl.ds(N_META - j, tm), cols]
        count = jnp.minimum(pos + 1, w).astype(f32)
        pooled = acc / count - ext_ref[N_META:, cols]
        mixed = jnp.dot(pooled.astype(bf16), pw_ref[g], preferred_element_type=f32)
        o_ref[:, cols] = (mixed * ps_ref[:, cols]).astype(o_ref.dtype)


def _pool(u3, u_meta, pool_w, pool_scale, tm):
    b, s, _ = u3.shape
    per = tm // N_META
    return pl.pallas_call(
        functools.partial(_pool_kernel, tm=tm),
        grid=(b, s // tm),
        in_specs=[pl.BlockSpec((None, tm, POOL_WIDTH), lambda bb, i: (bb, i, 0)),
                  pl.BlockSpec((None, N_META, POOL_WIDTH),
                               lambda bb, i: (bb, jnp.maximum(i * per - 1, 0), 0)),
                  pl.BlockSpec((N_META, POOL_WIDTH), lambda bb, i: (0, 0)),
                  pl.BlockSpec((N_POOL_GROUPS, POOL_GROUP_DIM, POOL_GROUP_DIM), lambda bb, i: (0, 0, 0)),
                  pl.BlockSpec((1, POOL_WIDTH), lambda bb, i: (0, 0))],
        out_specs=pl.BlockSpec((None, tm, POOL_WIDTH), lambda bb, i: (bb, i, 0)),
        out_shape=jax.ShapeDtypeStruct((b, s, POOL_WIDTH), bf16),
        scratch_shapes=[pltpu.VMEM((tm + N_META, POOL_WIDTH), f32)],
        compiler_params=_cparams(("parallel", "parallel")),
        name="pool_mixer",
    )(u3, u3, u_meta, pool_w, pool_scale)


def _order_key(score):
    bits = pltpu.bitcast(score + 0.0, i32)
    return jnp.where(bits < 0, bits ^ jnp.int32(0x7FFFFFFF), bits)


def _attn_kernel(q_ref, iq_ref, iw_ref, kT_ref, v_ref, ikT_ref, kTm_ref, vm_ref, ikTm_ref,
                 o_ref, key_ref, *, n_sel, n_chunks, ord_bits):
    qb = q_ref.shape[0]
    kc = qb
    meta = n_chunks
    qi = pl.program_id(1)
    n_ch = qi + 1
    row = qi * qb + lax.broadcasted_iota(i32, (qb, 1), 0)
    lane_k = lax.broadcasted_iota(i32, (1, kc), 1)

    iq_heads = [iq_ref[:, h * IDX_DIM:(h + 1) * IDX_DIM] for h in range(N_IDX_HEADS)]
    iw = iw_ref[...]
    iw_cols = [iw[:, h:h + 1] for h in range(N_IDX_HEADS)]

    def index_keys(ikT_c, valid):
        acc = jnp.zeros((qb, kc), f32)
        for h in range(N_IDX_HEADS):
            lg = jnp.dot(iq_heads[h], ikT_c, preferred_element_type=f32)
            acc = acc + jnp.maximum(lg, 0.0) * iw_cols[h]
        return jnp.where(valid, _order_key(acc), INT_MIN)

    def score_body(c, carry):
        key_ref[c] = index_keys(ikT_ref[c], c * kc + lane_k <= row)
        return carry

    lax.fori_loop(0, n_ch, score_body, 0)
    key_ref[meta] = index_keys(ikTm_ref[...], lane_k < N_META)

    def count_rows(pred_fn):
        def add(c, acc):
            hit = pred_fn(c, key_ref[c])
            for j in range(kc // LANES):
                acc = acc + jnp.where(hit[:, j * LANES:(j + 1) * LANES], 1.0, 0.0)
            return acc
        acc = lax.fori_loop(0, n_ch, add, jnp.zeros((qb, LANES), f32))
        acc = add(meta, acc)
        return jnp.sum(acc, axis=1, keepdims=True)

    def count_ge(cand):
        return count_rows(lambda c, k: k >= cand)

    t0 = jnp.where(count_ge(jnp.zeros((qb, 1), i32)) >= n_sel, 0, INT_MIN).astype(i32)

    def bisect(it, t):
        cand = t + lax.shift_left(jnp.int32(1), 30 - it)
        return jnp.where(count_ge(cand) >= n_sel, cand, t)

    thr = lax.fori_loop(0, 31, bisect, t0)
    n_ge = count_ge(thr)
    tied = jnp.logical_and(n_ge > n_sel, thr > INT_MIN)
    any_tied = jnp.max(jnp.where(tied, 1.0, 0.0)) > 0.0
    floor_key = jnp.maximum(thr, INT_MIN + 1)

    def ordinal(c):
        return jnp.where(c == meta, lane_k, c * kc + lane_k + N_META)

    @pl.when(jnp.logical_not(any_tied))
    def _():
        def to_bias(c, carry):
            key_ref[c] = pltpu.bitcast(jnp.where(key_ref[c] >= floor_key, 0.0, NEG_BIG).astype(f32), i32)
            return carry
        lax.fori_loop(0, n_ch, to_bias, 0)
        to_bias(meta, 0)

    @pl.when(any_tied)
    def _():
        need = n_sel - count_ge(thr + 1)

        def bisect_pos(it, lo):
            cand = lo + lax.shift_left(jnp.int32(1), ord_bits - 1 - it)
            cnt = count_rows(lambda c, k: jnp.logical_and(k == thr, ordinal(c) <= cand))
            return jnp.where(cnt < need, cand, lo)

        lo = lax.fori_loop(0, ord_bits, bisect_pos, jnp.full((qb, 1), -1, i32))
        last = jnp.where(tied, lo + 1, jnp.int32(2 ** 30))

        def to_bias(c, carry):
            k = key_ref[c]
            keep = jnp.logical_and(k >= floor_key, jnp.logical_or(k != thr, ordinal(c) <= last))
            key_ref[c] = pltpu.bitcast(jnp.where(keep, 0.0, NEG_BIG).astype(f32), i32)
            return carry
        lax.fori_loop(0, n_ch, to_bias, 0)
        to_bias(meta, 0)

    for hp in range(N_HEADS // 2):
        outs = []
        for h in (2 * hp, 2 * hp + 1):
            hs = slice(h * HEAD_DIM, (h + 1) * HEAD_DIM)
            q_h = q_ref[:, hs]

            def step(kT_h, v_h, bias, carry):
                m, l, acc = carry
                s = jnp.dot(q_h, kT_h, preferred_element_type=f32) + bias
                m_new = jnp.maximum(m, jnp.max(s, axis=1, keepdims=True))
                alpha = jnp.exp(m - m_new)
                p = jnp.exp(s - m_new)
                l = alpha * l + jnp.sum(p, axis=1, keepdims=True)
                acc = alpha * acc + jnp.dot(p.astype(bf16), v_h, preferred_element_type=f32)
                return m_new, l, acc

            def body(c, carry):
                start = pl.multiple_of(c * kc, kc)
                return step(kT_ref[c, hs, :], v_ref[pl.ds(start, kc), hs],
                            pltpu.bitcast(key_ref[c], f32), carry)

            carry = (jnp.full((qb, 1), NEG_BIG, f32), jnp.zeros((qb, 1), f32),
                     jnp.zeros((qb, HEAD_DIM), f32))
            carry = lax.fori_loop(0, n_ch, body, carry)
            m, l, acc = step(kTm_ref[hs, :], vm_ref[:, hs], pltpu.bitcast(key_ref[meta], f32), carry)
            outs.append(acc / l)
        o_ref[:, 2 * hp * HEAD_DIM:(2 * hp + 2) * HEAD_DIM] = jnp.concatenate(outs, axis=1).astype(o_ref.dtype)


def _attention(q, iq, iw, kT, v, ikT, kT_meta, v_meta, ikT_meta, batch, seq, n_sel):
    qb = ATTN_QB
    n_chunks = seq // qb
    ord_bits = max(1, (seq + N_META - 1).bit_length())
    kernel = functools.partial(_attn_kernel, n_sel=n_sel, n_chunks=n_chunks, ord_bits=ord_bits)
    once = pl.Buffered(1)
    return pl.pallas_call(
        kernel,
        grid=(batch, n_chunks),
        in_specs=[pl.BlockSpec((qb, ATTN_WIDTH), lambda b, i: (b * n_chunks + i, 0)),
                  pl.BlockSpec((qb, N_IDX_HEADS * IDX_DIM), lambda b, i: (b * n_chunks + i, 0)),
                  pl.BlockSpec((qb, LANES), lambda b, i: (b * n_chunks + i, 0)),
                  pl.BlockSpec((None, n_chunks, ATTN_WIDTH, qb), lambda b, i: (b, 0, 0, 0), pipeline_mode=once),
                  pl.BlockSpec((seq, ATTN_WIDTH), lambda b, i: (b, 0), pipeline_mode=once),
                  pl.BlockSpec((None, n_chunks, IDX_DIM, qb), lambda b, i: (b, 0, 0, 0), pipeline_mode=once),
                  pl.BlockSpec((ATTN_WIDTH, qb), lambda b, i: (0, 0)),
                  pl.BlockSpec((qb, ATTN_WIDTH), lambda b, i: (0, 0)),
                  pl.BlockSpec((IDX_DIM, qb), lambda b, i: (0, 0))],
        out_specs=pl.BlockSpec((qb, ATTN_WIDTH), lambda b, i: (b * n_chunks + i, 0)),
        out_shape=jax.ShapeDtypeStruct((batch * seq, ATTN_WIDTH), bf16),
        scratch_shapes=[pltpu.VMEM((n_chunks + 1, qb, qb), i32)],
        compiler_params=_cparams(("parallel", "arbitrary")),
        name="indexed_attention",
    )(q, iq, iw, kT, v, ikT, kT_meta, v_meta, ikT_meta)


def _post_kernel(x_ref, pp_ref, at_ref, sgp_ref, sga_ref, wbp_ref, wba_ref, wout_ref, g_ref, rw_ref, rb_ref,
                 h1_ref, hn_ref, ri_ref, rg_ref, cnt_ref, carry_ref):
    tm = x_ref.shape[0]

    @pl.when(pl.program_id(0) == 0)
    def _():
        carry_ref[...] = jnp.zeros_like(carry_ref)

    y_pool = jnp.dot(pp_ref[...], wbp_ref[...], preferred_element_type=f32)
    y_attn = jnp.dot(at_ref[...], wba_ref[...], preferred_element_type=f32)
    merged = sgp_ref[...].astype(f32) * y_pool + sga_ref[...].astype(f32) * y_attn
    h1 = x_ref[...] + jnp.dot(merged.astype(bf16), wout_ref[...], preferred_element_type=f32)
    h1_ref[...] = h1
    hn = h1 * lax.rsqrt(jnp.mean(h1 * h1, axis=-1, keepdims=True) + EPS) * g_ref[...]
    hn_ref[...] = hn

    lane = lax.broadcasted_iota(i32, (tm, LANES), 1)
    logits = jnp.dot(hn, rw_ref[...], preferred_element_type=f32, precision=lax.Precision.HIGHEST) + rb_ref[...]
    logits = jnp.where(lane < N_EXPERTS, logits, -jnp.inf)
    vals, idxs = [], []
    for _ in range(TOP_K):
        m = jnp.max(logits, axis=1, keepdims=True)
        idx = jnp.min(jnp.where(logits == m, lane, LANES), axis=1, keepdims=True)
        vals.append(m)
        idxs.append(idx)
        logits = jnp.where(lane == idx, -jnp.inf, logits)
    exps = [jnp.exp(v - vals[0]) for v in vals]
    denom = exps[0] + exps[1] + exps[2] + exps[3]

    hot = jnp.zeros((tm, LANES), f32)
    for idx in idxs:
        hot = hot + jnp.where(lane == idx, 1.0, 0.0)
    r_i = lax.broadcasted_iota(i32, (tm, tm), 0)
    c_i = lax.broadcasted_iota(i32, (tm, tm), 1)
    below = jnp.where(c_i < r_i, 1.0, 0.0).astype(bf16)
    before = jnp.dot(below, hot.astype(bf16), preferred_element_type=f32) + carry_ref[...]
    ri = jnp.zeros((tm, LANES), i32)
    rg = jnp.zeros((tm, LANES), f32)
    for k in range(TOP_K):
        rank = jnp.sum(jnp.where(lane == idxs[k], before, 0.0), axis=1, keepdims=True).astype(i32)
        ri = jnp.where(lane == k, idxs[k], ri)
        ri = jnp.where(lane == TOP_K + k, rank, ri)
        rg = jnp.where(lane == k, exps[k] / denom, rg)
    ri_ref[...] = ri
    rg_ref[...] = rg
    carry_ref[...] = carry_ref[...] + jnp.sum(hot, axis=0, keepdims=True)
    cnt_ref[...] = carry_ref[...]


def _post(x2d, pp, at, sgp, sga, wbp, wba, wout, g_moe, rw, rb, tm):
    n = x2d.shape[0]
    once = pl.Buffered(1)
    row = lambda w: pl.BlockSpec((tm, w), lambda i: (i, 0))
    full = lambda a, c: pl.BlockSpec((a, c), lambda i: (0, 0), pipeline_mode=once)
    return pl.pallas_call(
        _post_kernel,
        grid=(n // tm,),
        in_specs=[row(D_MODEL), row(POOL_WIDTH), row(ATTN_WIDTH), row(D_MODEL),
                  pl.BlockSpec((tm, D_MODEL), lambda i: (i, 1)),
                  full(POOL_WIDTH, D_MODEL), full(ATTN_WIDTH, D_MODEL), full(D_MODEL, D_MODEL),
                  full(1, D_MODEL), full(D_MODEL, LANES), full(1, LANES)],
        out_specs=[row(D_MODEL), row(D_MODEL), row(LANES), row(LANES),
                   pl.BlockSpec((1, LANES), lambda i: (0, 0))],
        out_shape=[jax.ShapeDtypeStruct((n, D_MODEL), f32), jax.ShapeDtypeStruct((n, D_MODEL), f32),
                   jax.ShapeDtypeStruct((n, LANES), i32), jax.ShapeDtypeStruct((n, LANES), f32),
                   jax.ShapeDtypeStruct((1, LANES), f32)],
        scratch_shapes=[pltpu.VMEM((1, LANES), f32)],
        compiler_params=_cparams(("arbitrary",)),
        name="merge_outproj_router",
    )(x2d, pp, at, sgp, sga, wbp, wba, wout, g_moe, rw, rb)


def _dispatch_kernel(dest_ref, hn_hbm, xs_in_hbm, xs_hbm, sem, *, tm):
    del xs_in_hbm
    base = pl.program_id(0) * tm

    def row_copy(j, k):
        return pltpu.make_async_copy(hn_hbm.at[pl.ds(base + j, 1)],
                                     xs_hbm.at[pl.ds(dest_ref[j * TOP_K + k], 1)], sem)

    def issue(j, carry):
        for k in range(TOP_K):
            row_copy(j, k).start()
        return carry

    def drain(j, carry):
        for k in range(TOP_K):
            row_copy(j, k).wait()
        return carry

    lax.fori_loop(0, tm, issue, 0)
    lax.fori_loop(0, tm, drain, 0)


def _dispatch(dest_flat, hn, xs_init, tm):
    n = hn.shape[0]
    return pl.pallas_call(
        functools.partial(_dispatch_kernel, tm=tm),
        grid=(n // tm,),
        in_specs=[pl.BlockSpec((tm * TOP_K,), lambda i: (i,), memory_space=pltpu.SMEM),
                  pl.BlockSpec(memory_space=pl.ANY),
                  pl.BlockSpec(memory_space=pl.ANY)],
        out_specs=pl.BlockSpec(memory_space=pl.ANY),
        out_shape=jax.ShapeDtypeStruct(xs_init.shape, xs_init.dtype),
        scratch_shapes=[pltpu.SemaphoreType.DMA],
        input_output_aliases={2: 0},
        compiler_params=pltpu.CompilerParams(dimension_semantics=("arbitrary",), has_side_effects=True),
        name="moe_dispatch",
    )(dest_flat, hn, xs_init)


def _expert_kernel(be_ref, nu_ref, xs_ref, wg_ref, wu_ref, wd_ref, bg_ref, bu_ref, bd_ref, y_ref, xb_ref):
    b, f = pl.program_id(0), pl.program_id(1)
    used = b < nu_ref[0]

    @pl.when(jnp.logical_and(used, f == 0))
    def _():
        xb_ref[...] = xs_ref[...].astype(bf16)

    @pl.when(used)
    def _():
        x = xb_ref[...]
        gt = jnp.dot(x, wg_ref[...].astype(bf16), preferred_element_type=f32) + bg_ref[...]
        up = jnp.dot(x, wu_ref[...].astype(bf16), preferred_element_type=f32) + bu_ref[...]
        gt = jnp.minimum(gt, SWIGLU_LIMIT)
        up = jnp.clip(up, -SWIGLU_LIMIT, SWIGLU_LIMIT)
        hidden = (up + 1.0) * (gt * jax.nn.sigmoid(SWIGLU_ALPHA * gt))
        part = jnp.dot(hidden.astype(bf16), wd_ref[...].astype(bf16), preferred_element_type=f32)

        @pl.when(f == 0)
        def _():
            y_ref[...] = part + bd_ref[...]

        @pl.when(f > 0)
        def _():
            y_ref[...] += part

    @pl.when(jnp.logical_and(jnp.logical_not(used), f == 0))
    def _():
        y_ref[...] = jnp.zeros_like(y_ref)


def _experts(block_expert, n_used, xs, wg, wu, wd, bg, bu, bd, bm, tf):
    n_rows = xs.shape[0]
    n_blocks = n_rows // bm
    n_f = D_FF // tf

    def blk(b, nu):
        return jnp.minimum(b, nu[0] - 1)

    def ftile(b, f, nu):
        return jnp.where(b < nu[0], f, n_f - 1)

    grid_spec = pltpu.PrefetchScalarGridSpec(
        num_scalar_prefetch=2,
        grid=(n_blocks, n_f),
        in_specs=[pl.BlockSpec((bm, D_MODEL), lambda b, f, be, nu: (blk(b, nu), 0)),
                  pl.BlockSpec((None, D_MODEL, tf), lambda b, f, be, nu: (be[blk(b, nu)], 0, ftile(b, f, nu))),
                  pl.BlockSpec((None, D_MODEL, tf), lambda b, f, be, nu: (be[blk(b, nu)], 0, ftile(b, f, nu))),
                  pl.BlockSpec((None, tf, D_MODEL), lambda b, f, be, nu: (be[blk(b, nu)], ftile(b, f, nu), 0)),
                  pl.BlockSpec((None, 1, tf), lambda b, f, be, nu: (be[blk(b, nu)], 0, ftile(b, f, nu))),
                  pl.BlockSpec((None, 1, tf), lambda b, f, be, nu: (be[blk(b, nu)], 0, ftile(b, f, nu))),
                  pl.BlockSpec((None, 1, D_MODEL), lambda b, f, be, nu: (be[blk(b, nu)], 0, 0))],
        out_specs=pl.BlockSpec((bm, D_MODEL), lambda b, f, be, nu: (b, 0)),
        scratch_shapes=[pltpu.VMEM((bm, D_MODEL), bf16)],
    )
    return pl.pallas_call(
        _expert_kernel,
        grid_spec=grid_spec,
        out_shape=jax.ShapeDtypeStruct((n_rows, D_MODEL), f32),
        compiler_params=_cparams(("arbitrary", "arbitrary")),
        name="moe_experts",
    )(block_expert, n_used, xs, wg, wu, wd, bg, bu, bd)


def _combine_kernel(dest_ref, y_hbm, rg_ref, h1_ref, g_ref, o_ref, buf_ref, sem, *, tm):
    def row_copy(j, k):
        return pltpu.make_async_copy(y_hbm.at[pl.ds(dest_ref[j * TOP_K + k], 1)],
                                     buf_ref.at[k, pl.ds(j, 1)], sem)

    def issue(j, carry):
        for k in range(TOP_K):
            row_copy(j, k).start()
        return carry

    def drain(j, carry):
        for k in range(TOP_K):
            row_copy(j, k).wait()
        return carry

    lax.fori_loop(0, tm, issue, 0)
    lax.fori_loop(0, tm, drain, 0)
    rg = rg_ref[...]
    h = h1_ref[...]
    for k in range(TOP_K):
        h = h + buf_ref[k] * rg[:, k:k + 1]
    o_ref[...] = h * lax.rsqrt(jnp.mean(h * h, axis=-1, keepdims=True) + EPS) * g_ref[...]


def _combine(dest_flat, y, rg, h1, g_fin, tm):
    n = h1.shape[0]
    return pl.pallas_call(
        functools.partial(_combine_kernel, tm=tm),
        grid=(n // tm,),
        in_specs=[pl.BlockSpec((tm * TOP_K,), lambda i: (i,), memory_space=pltpu.SMEM),
                  pl.BlockSpec(memory_space=pl.ANY),
                  pl.BlockSpec((tm, LANES), lambda i: (i, 0)),
                  pl.BlockSpec((tm, D_MODEL), lambda i: (i, 0)),
                  pl.BlockSpec((1, D_MODEL), lambda i: (0, 0))],
        out_specs=pl.BlockSpec((tm, D_MODEL), lambda i: (i, 0)),
        out_shape=jax.ShapeDtypeStruct((n, D_MODEL), f32),
        scratch_shapes=[pltpu.VMEM((TOP_K, tm, D_MODEL), f32), pltpu.SemaphoreType.DMA],
        compiler_params=_cparams(("arbitrary",)),
        name="moe_combine_norm",
    )(dest_flat, y, rg, h1, g_fin)


def _rope_tables(n_pos):
    freqs = ROPE_THETA ** (-jnp.arange(0, ROT_DIM, 2, dtype=f32) / ROT_DIM)
    ang = jnp.arange(n_pos, dtype=f32)[:, None] * freqs[None, :]
    cos, sin = jnp.cos(ang), jnp.sin(ang)
    ones = jnp.ones((n_pos, HEAD_DIM - ROT_DIM), f32)
    zeros8 = jnp.zeros((n_pos, ROT_HALF), f32)
    zeros48 = jnp.zeros((n_pos, HEAD_DIM - ROT_DIM), f32)
    c = jnp.concatenate([cos, cos, ones], axis=1)
    s1 = jnp.concatenate([zeros8, sin, zeros48], axis=1)
    s2 = jnp.concatenate([-sin, zeros8, zeros48], axis=1)
    tile2 = lambda t: jnp.concatenate([t, t], axis=1)
    return (tile2(c), tile2(s1), tile2(s2)), (cos.T, sin.T)


def kernel(x, meta_tokens, norm_mix_g, w_in, pool_w, pool_scale, w_branch_pool, w_branch_attn, w_out,
           norm_moe_g, router_w, router_b, exp_w_gate, exp_b_gate, exp_w_up, exp_b_up, exp_w_down,
           exp_b_down, norm_final_g):
    batch, seq, _ = x.shape
    n_tok = batch * seq
    n_sel = min(TOPK_KEYS_MAX, (seq + N_META) // 4)
    x2d = x.reshape(n_tok, D_MODEL)
    qb = ATTN_QB

    w = w_in[0]
    o_u, o_q, o_k, o_v, o_iq = 0, POOL_WIDTH, POOL_WIDTH + ATTN_WIDTH, POOL_WIDTH + 2 * ATTN_WIDTH, POOL_WIDTH + 3 * ATTN_WIDTH
    o_ik = o_iq + N_IDX_HEADS * IDX_DIM
    o_iw = o_ik + IDX_DIM
    o_gp = o_iw + N_IDX_HEADS
    o_ga = o_gp + D_MODEL
    cast = lambda a: a.astype(bf16)
    w_u, w_q, w_v, w_iq = (cast(w[:, o:o + 1024]) for o in (o_u, o_q, o_v, o_iq))
    w_kT = cast(w[:, o_k:o_k + ATTN_WIDTH].T)
    w_ikT = cast(w[:, o_ik:o_ik + IDX_DIM].T)
    w_iw = cast(jnp.pad(w[:, o_iw:o_iw + N_IDX_HEADS], ((0, 0), (0, LANES - N_IDX_HEADS)))
                * (N_IDX_HEADS ** -0.5 * IDX_DIM ** -0.5))
    w_gates = cast(w[:, o_gp:o_ga + D_MODEL])

    (tab_tok, (cosT, sinT)) = _rope_tables(seq + N_META)
    tok_x = tuple(t[N_META:] for t in tab_tok)
    cosT_x, sinT_x = cosT[:, N_META:], sinT[:, N_META:]
    pad_cols = lambda t, n: jnp.pad(t[:, :n] if t.shape[1] >= n else t, ((0, 0), (0, max(0, n - t.shape[1]))))
    cosT_m, sinT_m = pad_cols(cosT, qb), pad_cols(sinT, qb)

    xn = _rmsnorm(x2d, norm_mix_g[0], NORM_TM)
    meta_pad = jnp.pad(meta_tokens.astype(f32), ((0, qb - N_META), (0, 0)))
    xn_meta = _rmsnorm(meta_pad, norm_mix_g[0], qb)

    u = _proj(xn, w_u, PROJ_TM, f32)
    v = _proj(xn, w_v, PROJ_TM, bf16)
    sg = _proj(xn, w_gates, PROJ_TM, bf16, act="sigmoid")
    iw = _proj(xn, w_iw, PROJ_TM, f32)
    q = _proj_rope(xn, w_q, tok_x, PROJ_TM, seq // PROJ_TM, HEAD_DIM ** -0.5)
    iq = _proj_rope(xn, w_iq, tok_x, PROJ_TM, seq // PROJ_TM, 1.0)
    kT = _projT_rope(xn, w_kT, cosT_x, sinT_x, qb, seq // qb, N_HEADS)
    ikT = _projT_rope(xn, w_ikT, cosT_x, sinT_x, qb, seq // qb, 1)

    u_meta = _proj(xn_meta, w_u, qb, f32)[:N_META]
    v_meta = _proj(xn_meta, w_v, qb, bf16)
    kT_meta = _projT_rope(xn_meta, w_kT, cosT_m, sinT_m, qb, 1, N_HEADS)[0]
    ikT_meta = _projT_rope(xn_meta, w_ikT, cosT_m, sinT_m, qb, 1, 1)[0]

    pp = _pool(u.reshape(batch, seq, POOL_WIDTH), u_meta, cast(pool_w[0]), pool_scale[0].reshape(1, POOL_WIDTH),
               POOL_TM).reshape(n_tok, POOL_WIDTH)
    at = _attention(q, iq, iw, kT.reshape(batch, seq // qb, ATTN_WIDTH, qb), v,
                    ikT.reshape(batch, seq // qb, IDX_DIM, qb), kT_meta, v_meta, ikT_meta, batch, seq, n_sel)

    rw = jnp.pad(router_w[0].astype(f32), ((0, 0), (0, LANES - N_EXPERTS)))
    rb = jnp.pad(router_b[0].astype(f32), (0, LANES - N_EXPERTS)).reshape(1, LANES)
    h1, hn, ri, rg, cnt = _post(x2d, pp, at, sg, sg, cast(w_branch_pool[0]),
                                cast(w_branch_attn[0]), cast(w_out[0]), norm_moe_g[0].reshape(1, D_MODEL),
                                rw, rb, POST_TM)

    bm = MOE_BM
    counts = cnt[0, :N_EXPERTS].astype(i32)
    padded = ((counts + bm - 1) // bm) * bm
    pad_end = jnp.cumsum(padded)
    pad_start = pad_end - padded
    n_blocks = -(-(n_tok * TOP_K + N_EXPERTS * (bm - 1)) // bm)
    dest = (pad_start[ri[:, :TOP_K]] + ri[:, TOP_K:2 * TOP_K]).reshape(n_tok * TOP_K)
    block_expert = jnp.minimum(jnp.searchsorted(pad_end, jnp.arange(n_blocks, dtype=i32) * bm, side="right"),
                               N_EXPERTS - 1).astype(i32)
    n_used = (pad_end[-1] // bm).astype(i32).reshape(1)

    xs = _dispatch(dest, hn, jnp.zeros((n_blocks * bm, D_MODEL), f32), DISPATCH_TM)
    y = _experts(block_expert, n_used, xs, exp_w_gate[0], exp_w_up[0], exp_w_down[0],
                 exp_b_gate[0].reshape(N_EXPERTS, 1, D_FF), exp_b_up[0].reshape(N_EXPERTS, 1, D_FF),
                 exp_b_down[0].reshape(N_EXPERTS, 1, D_MODEL), bm, MOE_TF)
    out = _combine(dest, y, rg, h1, norm_final_g.reshape(1, D_MODEL), COMBINE_TM)
    return out.reshape(batch, seq, D_MODEL)
```

```python
import functools

import jax
import jax.numpy as jnp
from jax import lax
from jax.experimental import pallas as pl
from jax.experimental.pallas import tpu as pltpu

f32 = jnp.float32
bf16 = jnp.bfloat16
i32 = jnp.int32

D_MODEL = 2048
N_META = 16
POOL_WINDOWS = (2, 4, 8, 16)
N_POOL_GROUPS = 4
POOL_WIDTH = D_MODEL // 2
POOL_GROUP_DIM = POOL_WIDTH // N_POOL_GROUPS
N_HEADS = 16
HEAD_DIM = 64
ATTN_WIDTH = N_HEADS * HEAD_DIM
ROT_DIM = HEAD_DIM // 4
ROT_HALF = ROT_DIM // 2
ROPE_THETA = 500000.0
N_IDX_HEADS = 16
IDX_DIM = 64
TOPK_KEYS_MAX = 256
N_EXPERTS = 32
TOP_K = 4
D_FF = D_MODEL
SWIGLU_LIMIT = 7.0
SWIGLU_ALPHA = 1.702
EPS = 1e-5

LANES = 128
VMEM_LIMIT = 56 * 1024 * 1024

NORM_TM = 512
PROJ_TM = 512
ATTN_QB = 256
POOL_TM = 512
POST_TM = 256
MOE_BM = 1024
MOE_TF = 256
COMBINE_TM = 256

NEG_BIG = -1e30
INT_MIN = -(2 ** 31)


def _cparams(sem):
    return pltpu.CompilerParams(dimension_semantics=sem, vmem_limit_bytes=VMEM_LIMIT)


def _rmsnorm_kernel(x_ref, g_ref, o_ref):
    x = x_ref[...]
    ms = jnp.mean(x * x, axis=-1, keepdims=True)
    o_ref[...] = (x * lax.rsqrt(ms + EPS) * g_ref[...]).astype(o_ref.dtype)


def _rmsnorm(x2d, g, tm):
    n = x2d.shape[0]
    return pl.pallas_call(
        _rmsnorm_kernel,
        grid=(n // tm,),
        in_specs=[pl.BlockSpec((tm, D_MODEL), lambda i: (i, 0)),
                  pl.BlockSpec((1, D_MODEL), lambda i: (0, 0))],
        out_specs=pl.BlockSpec((tm, D_MODEL), lambda i: (i, 0)),
        out_shape=jax.ShapeDtypeStruct((n, D_MODEL), bf16),
        compiler_params=_cparams(("parallel",)),
        name="rmsnorm",
    )(x2d, g.reshape(1, D_MODEL))


def _proj_kernel(xn_ref, w_ref, o_ref, *, act):
    y = jnp.dot(xn_ref[...], w_ref[...], preferred_element_type=f32)
    if act == "sigmoid":
        y = jax.nn.sigmoid(y)
    o_ref[...] = y.astype(o_ref.dtype)


def _proj(xn, w, tm, out_dtype, act=None, tn=1024):
    n, width = xn.shape[0], w.shape[1]
    tn = min(tn, width)
    return pl.pallas_call(
        functools.partial(_proj_kernel, act=act),
        grid=(width // tn, n // tm),
        in_specs=[pl.BlockSpec((tm, D_MODEL), lambda j, i: (i, 0)),
                  pl.BlockSpec((D_MODEL, tn), lambda j, i: (0, j))],
        out_specs=pl.BlockSpec((tm, tn), lambda j, i: (i, j)),
        out_shape=jax.ShapeDtypeStruct((n, width), out_dtype),
        compiler_params=_cparams(("parallel", "parallel")),
        name="proj_" + (act or "plain"),
    )(xn, w)


def _proj_rope_kernel(xn_ref, w_ref, c_ref, s1_ref, s2_ref, o_ref, *, scale):
    y = jnp.dot(xn_ref[...], w_ref[...], preferred_element_type=f32)
    c, s1, s2 = c_ref[...], s1_ref[...], s2_ref[...]
    for j in range(y.shape[1] // LANES):
        yt = y[:, j * LANES:(j + 1) * LANES]
        r = yt * c + pltpu.roll(yt, ROT_HALF, 1) * s1 + pltpu.roll(yt, LANES - ROT_HALF, 1) * s2
        o_ref[:, j * LANES:(j + 1) * LANES] = (r * scale).astype(o_ref.dtype)


def _proj_rope(xn, w, tables, tm, pos_blocks, scale):
    n, width = xn.shape[0], w.shape[1]
    tspec = pl.BlockSpec((tm, LANES), lambda i: (i % pos_blocks, 0))
    return pl.pallas_call(
        functools.partial(_proj_rope_kernel, scale=scale),
        grid=(n // tm,),
        in_specs=[pl.BlockSpec((tm, D_MODEL), lambda i: (i, 0)),
                  pl.BlockSpec((D_MODEL, width), lambda i: (0, 0)),
                  tspec, tspec, tspec],
        out_specs=pl.BlockSpec((tm, width), lambda i: (i, 0)),
        out_shape=jax.ShapeDtypeStruct((n, width), bf16),
        compiler_params=_cparams(("parallel",)),
        name="proj_rope",
    )(xn, w, *tables)


def _projT_rope_kernel(wT_ref, xn_ref, cos_ref, sin_ref, o_ref, *, n_heads):
    y = lax.dot_general(wT_ref[...], xn_ref[...], (((1,), (1,)), ((), ())),
                        preferred_element_type=f32)
    c, s = cos_ref[...], sin_ref[...]
    for h in range(n_heads):
        blk = y[h * HEAD_DIM:(h + 1) * HEAD_DIM]
        x1, x2 = blk[0:ROT_HALF], blk[ROT_HALF:ROT_DIM]
        out = jnp.concatenate([x1 * c - x2 * s, x2 * c + x1 * s, blk[ROT_DIM:]], axis=0)
        o_ref[h * HEAD_DIM:(h + 1) * HEAD_DIM, :] = out.astype(o_ref.dtype)


def _projT_rope(xn, wT, cosT, sinT, tm, pos_blocks, n_heads):
    n, rows = xn.shape[0], wT.shape[0]
    tspec = pl.BlockSpec((ROT_HALF, tm), lambda i: (0, i % pos_blocks))
    return pl.pallas_call(
        functools.partial(_projT_rope_kernel, n_heads=n_heads),
        grid=(n // tm,),
        in_specs=[pl.BlockSpec((rows, D_MODEL), lambda i: (0, 0)),
                  pl.BlockSpec((tm, D_MODEL), lambda i: (i, 0)),
                  tspec, tspec],
        out_specs=pl.BlockSpec((None, rows, tm), lambda i: (i, 0, 0)),
        out_shape=jax.ShapeDtypeStruct((n // tm, rows, tm), bf16),
        compiler_params=_cparams(("parallel",)),
        name="projT_rope",
    )(wT, xn, cosT, sinT)


def _pool_kernel(u_ref, prev_ref, meta_ref, pw_ref, ps_ref, o_ref, ext_ref, *, tm):
    i = pl.program_id(1)
    halo = jnp.where(i == 0, meta_ref[...], prev_ref[...])
    ext_ref[0:N_META, :] = halo
    ext_ref[N_META:, :] = u_ref[...]
    pos = i * tm + lax.broadcasted_iota(i32, (tm, 1), 0) + N_META
    for g, w in enumerate(POOL_WINDOWS):
        cols = slice(g * POOL_GROUP_DIM, (g + 1) * POOL_GROUP_DIM)
        acc = ext_ref[N_META:, cols]
        for j in range(1, w):
            acc = acc + ext_ref[pl.ds(N_META - j, tm), cols]
        count = jnp.minimum(pos + 1, w).astype(f32)
        pooled = acc / count - ext_ref[N_META:, cols]
        mixed = jnp.dot(pooled.astype(bf16), pw_ref[g], preferred_element_type=f32)
        o_ref[:, cols] = (mixed * ps_ref[:, cols]).astype(o_ref.dtype)


def _pool(u3, u_meta, pool_w, pool_scale, tm):
    b, s, _ = u3.shape
    per = tm // N_META
    return pl.pallas_call(
        functools.partial(_pool_kernel, tm=tm),
        grid=(b, s // tm),
        in_specs=[pl.BlockSpec((None, tm, POOL_WIDTH), lambda bb, i: (bb, i, 0)),
                  pl.BlockSpec((None, N_META, POOL_WIDTH),
                               lambda bb, i: (bb, jnp.maximum(i * per - 1, 0), 0)),
                  pl.BlockSpec((N_META, POOL_WIDTH), lambda bb, i: (0, 0)),
                  pl.BlockSpec((N_POOL_GROUPS, POOL_GROUP_DIM, POOL_GROUP_DIM), lambda bb, i: (0, 0, 0)),
                  pl.BlockSpec((1, POOL_WIDTH), lambda bb, i: (0, 0))],
        out_specs=pl.BlockSpec((None, tm, POOL_WIDTH), lambda bb, i: (bb, i, 0)),
        out_shape=jax.ShapeDtypeStruct((b, s, POOL_WIDTH), bf16),
        scratch_shapes=[pltpu.VMEM((tm + N_META, POOL_WIDTH), f32)],
        compiler_params=_cparams(("parallel", "parallel")),
        name="pool_mixer",
    )(u3, u3, u_meta, pool_w, pool_scale)


def _order_key(score):
    bits = pltpu.bitcast(score + 0.0, i32)
    return jnp.where(bits < 0, bits ^ jnp.int32(0x7FFFFFFF), bits)


def _attn_kernel(q_ref, iq_ref, iw_ref, kT_ref, v_ref, ikT_ref, kTm_ref, vm_ref, ikTm_ref,
                 o_ref, key_ref, iwb_ref, m_ref, l_ref, acc_ref, *, n_sel, n_chunks, ord_bits):
    qb = q_ref.shape[0]
    kc = qb
    half = qb // 2
    meta = n_chunks
    qi = pl.program_id(1)
    n_ch = qi + 1
    row = qi * qb + lax.broadcasted_iota(i32, (qb, 1), 0)
    lane_k = lax.broadcasted_iota(i32, (1, kc), 1)

    iq_heads = [iq_ref[:, h * IDX_DIM:(h + 1) * IDX_DIM] for h in range(N_IDX_HEADS)]
    iw = iw_ref[...]
    for h in range(N_IDX_HEADS):
        iwb_ref[h] = jnp.broadcast_to(iw[:, h:h + 1], (qb, LANES))

    def index_keys(ikT_c, valid):
        acc = jnp.zeros((qb, kc), f32)
        for h in range(N_IDX_HEADS):
            lg = jnp.dot(iq_heads[h], ikT_c, preferred_element_type=f32)
            wgt = iwb_ref[h]
            acc = acc + jnp.maximum(lg, 0.0) * jnp.concatenate([wgt] * (kc // LANES), axis=1)
        return jnp.where(valid, _order_key(acc), INT_MIN)

    def score_body(c, carry):
        key_ref[c] = index_keys(ikT_ref[c], c * kc + lane_k <= row)
        return carry

    lax.fori_loop(0, n_ch, score_body, 0)
    key_ref[meta] = index_keys(ikTm_ref[...], lane_k < N_META)

    def count_rows(pred_fn, operands):
        outs = []
        for r in range(2):
            rows = slice(r * half, (r + 1) * half)
            ops = [jnp.broadcast_to(o[rows], (half, LANES)) for o in operands]

            def add(c, acc, rows=rows, ops=ops):
                k = key_ref[c, rows, :]
                for j in range(kc // LANES):
                    hit = pred_fn(c, j, k[:, j * LANES:(j + 1) * LANES], *ops)
                    acc = acc + jnp.where(hit, 1.0, 0.0)
                return acc
            acc = lax.fori_loop(0, n_ch, add, jnp.zeros((half, LANES), f32))
            acc = add(meta, acc)
            outs.append(jnp.sum(acc, axis=1, keepdims=True))
        return jnp.concatenate(outs, axis=0)

    def count_ge(cand):
        return count_rows(lambda c, j, k, cb: k >= cb, [cand])

    t0 = jnp.where(count_ge(jnp.zeros((qb, 1), i32)) >= n_sel, 0, INT_MIN).astype(i32)

    def bisect(it, t):
        cand = t + lax.shift_left(jnp.int32(1), 30 - it)
        return jnp.where(count_ge(cand) >= n_sel, cand, t)

    thr = lax.fori_loop(0, 31, bisect, t0)
    n_ge = count_ge(thr)
    tied = jnp.logical_and(n_ge > n_sel, thr > INT_MIN)
    any_tied = jnp.max(jnp.where(tied, 1.0, 0.0)) > 0.0
    floor_key = jnp.maximum(thr, INT_MIN + 1)

    def ordinal(c, lanes):
        return jnp.where(c == meta, lanes, c * kc + lanes + N_META)

    @pl.when(jnp.logical_not(any_tied))
    def _():
        def to_bias(c, carry):
            key_ref[c] = pltpu.bitcast(jnp.where(key_ref[c] >= floor_key, 0.0, NEG_BIG).astype(f32), i32)
            return carry
        lax.fori_loop(0, n_ch, to_bias, 0)
        to_bias(meta, 0)

    @pl.when(any_tied)
    def _():
        need = n_sel - count_ge(thr + 1)
        lane_t = lax.broadcasted_iota(i32, (1, LANES), 1)

        def tie_pred(c, j, k, tb, cb):
            return jnp.logical_and(k == tb, ordinal(c, lane_t + j * LANES) <= cb)

        def bisect_pos(it, lo):
            cand = lo + lax.shift_left(jnp.int32(1), ord_bits - 1 - it)
            cnt = count_rows(tie_pred, [thr, cand])
            return jnp.where(cnt < need, cand, lo)

        lo = lax.fori_loop(0, ord_bits, bisect_pos, jnp.full((qb, 1), -1, i32))
        last = jnp.where(tied, lo + 1, jnp.int32(2 ** 30))

        def to_bias(c, carry):
            k = key_ref[c]
            keep = jnp.logical_and(k >= floor_key, jnp.logical_or(k != thr, ordinal(c, lane_k) <= last))
            key_ref[c] = pltpu.bitcast(jnp.where(keep, 0.0, NEG_BIG).astype(f32), i32)
            return carry
        lax.fori_loop(0, n_ch, to_bias, 0)
        to_bias(meta, 0)

    q_heads = [q_ref[:, h * HEAD_DIM:(h + 1) * HEAD_DIM] for h in range(N_HEADS)]
    m_ref[...] = jnp.full(m_ref.shape, NEG_BIG, f32)
    l_ref[...] = jnp.zeros(l_ref.shape, f32)
    acc_ref[...] = jnp.zeros(acc_ref.shape, f32)
    low_half = lax.broadcasted_iota(i32, (1, 2 * HEAD_DIM), 1) < HEAD_DIM

    def attend(kT_c, v_c, bias):
        for hp in range(N_HEADS // 2):
            pair = slice(2 * hp * HEAD_DIM, (2 * hp + 2) * HEAD_DIM)
            v_pair = v_c[:, pair]
            alphas, parts = [], []
            for h in (2 * hp, 2 * hp + 1):
                s = jnp.dot(q_heads[h], kT_c[h * HEAD_DIM:(h + 1) * HEAD_DIM, :], preferred_element_type=f32) + bias
                m_old = m_ref[h]
                m_new = jnp.maximum(m_old, jnp.max(s, axis=1, keepdims=True))
                alpha = jnp.exp(m_old - m_new)
                p = jnp.exp(s - m_new)
                l_ref[h] = alpha * l_ref[h] + jnp.sum(p, axis=1, keepdims=True)
                m_ref[h] = m_new
                alphas.append(alpha)
                parts.append(jnp.dot(p.astype(bf16), v_pair, preferred_element_type=f32))
            acc_ref[:, pair] = (jnp.where(low_half, alphas[0], alphas[1]) * acc_ref[:, pair]
                                + jnp.where(low_half, parts[0], parts[1]))

    def attend_body(c, carry):
        start = pl.multiple_of(c * kc, kc)
        attend(kT_ref.at[c], v_ref.at[pl.ds(start, kc)], pltpu.bitcast(key_ref[c], f32))
        return carry

    lax.fori_loop(0, n_ch, attend_body, 0)
    attend(kTm_ref, vm_ref, pltpu.bitcast(key_ref[meta], f32))
    for hp in range(N_HEADS // 2):
        pair = slice(2 * hp * HEAD_DIM, (2 * hp + 2) * HEAD_DIM)
        denom = jnp.where(low_half, l_ref[2 * hp], l_ref[2 * hp + 1])
        o_ref[:, pair] = (acc_ref[:, pair] / denom).astype(o_ref.dtype)


def _attention(q, iq, iw, kT, v, ikT, kT_meta, v_meta, ikT_meta, batch, seq, n_sel):
    qb = ATTN_QB
    n_chunks = seq // qb
    ord_bits = max(1, (seq + N_META - 1).bit_length())
    kernel = functools.partial(_attn_kernel, n_sel=n_sel, n_chunks=n_chunks, ord_bits=ord_bits)
    once = pl.Buffered(1)
    return pl.pallas_call(
        kernel,
        grid=(batch, n_chunks),
        in_specs=[pl.BlockSpec((qb, ATTN_WIDTH), lambda b, i: (b * n_chunks + i, 0)),
                  pl.BlockSpec((qb, N_IDX_HEADS * IDX_DIM), lambda b, i: (b * n_chunks + i, 0)),
                  pl.BlockSpec((qb, LANES), lambda b, i: (b * n_chunks + i, 0)),
                  pl.BlockSpec((None, n_chunks, ATTN_WIDTH, qb), lambda b, i: (b, 0, 0, 0), pipeline_mode=once),
                  pl.BlockSpec((seq, ATTN_WIDTH), lambda b, i: (b, 0), pipeline_mode=once),
                  pl.BlockSpec((None, n_chunks, IDX_DIM, qb), lambda b, i: (b, 0, 0, 0), pipeline_mode=once),
                  pl.BlockSpec((ATTN_WIDTH, qb), lambda b, i: (0, 0)),
                  pl.BlockSpec((qb, ATTN_WIDTH), lambda b, i: (0, 0)),
                  pl.BlockSpec((IDX_DIM, qb), lambda b, i: (0, 0))],
        out_specs=pl.BlockSpec((qb, ATTN_WIDTH), lambda b, i: (b * n_chunks + i, 0)),
        out_shape=jax.ShapeDtypeStruct((batch * seq, ATTN_WIDTH), bf16),
        scratch_shapes=[pltpu.VMEM((n_chunks + 1, qb, qb), i32),
                        pltpu.VMEM((N_IDX_HEADS, qb, LANES), f32),
                        pltpu.VMEM((N_HEADS, qb, 1), f32),
                        pltpu.VMEM((N_HEADS, qb, 1), f32),
                        pltpu.VMEM((qb, ATTN_WIDTH), f32)],
        compiler_params=_cparams(("parallel", "arbitrary")),
        name="indexed_attention",
    )(q, iq, iw, kT, v, ikT, kT_meta, v_meta, ikT_meta)


def _post_kernel(x_ref, pp_ref, at_ref, sgp_ref, sga_ref, wbp_ref, wba_ref, wout_ref, g_ref, rw_ref, rb_ref,
                 h1_ref, hn_ref, ri_ref, rg_ref, cnt_ref, carry_ref):
    tm = x_ref.shape[0]

    @pl.when(pl.program_id(0) == 0)
    def _():
        carry_ref[...] = jnp.zeros_like(carry_ref)

    y_pool = jnp.dot(pp_ref[...], wbp_ref[...], preferred_element_type=f32)
    y_attn = jnp.dot(at_ref[...], wba_ref[...], preferred_element_type=f32)
    merged = sgp_ref[...].astype(f32) * y_pool + sga_ref[...].astype(f32) * y_attn
    h1 = x_ref[...] + jnp.dot(merged.astype(bf16), wout_ref[...], preferred_element_type=f32)
    h1_ref[...] = h1
    hn = h1 * lax.rsqrt(jnp.mean(h1 * h1, axis=-1, keepdims=True) + EPS) * g_ref[...]
    hn_ref[...] = hn

    lane = lax.broadcasted_iota(i32, (tm, LANES), 1)
    logits = jnp.dot(hn, rw_ref[...], preferred_element_type=f32, precision=lax.Precision.HIGHEST) + rb_ref[...]
    logits = jnp.where(lane < N_EXPERTS, logits, -jnp.inf)
    vals, idxs = [], []
    for _ in range(TOP_K):
        m = jnp.max(logits, axis=1, keepdims=True)
        idx = jnp.min(jnp.where(logits == m, lane, LANES), axis=1, keepdims=True)
        vals.append(m)
        idxs.append(idx)
        logits = jnp.where(lane == idx, -jnp.inf, logits)
    exps = [jnp.exp(v - vals[0]) for v in vals]
    denom = exps[0] + exps[1] + exps[2] + exps[3]

    hot = jnp.zeros((tm, LANES), f32)
    for idx in idxs:
        hot = hot + jnp.where(lane == idx, 1.0, 0.0)
    r_i = lax.broadcasted_iota(i32, (tm, tm), 0)
    c_i = lax.broadcasted_iota(i32, (tm, tm), 1)
    below = jnp.where(c_i < r_i, 1.0, 0.0).astype(bf16)
    before = jnp.dot(below, hot.astype(bf16), preferred_element_type=f32) + carry_ref[...]
    ri = jnp.zeros((tm, LANES), i32)
    rg = jnp.zeros((tm, LANES), f32)
    for k in range(TOP_K):
        rank = jnp.sum(jnp.where(lane == idxs[k], before, 0.0), axis=1, keepdims=True).astype(i32)
        ri = jnp.where(lane == k, idxs[k], ri)
        ri = jnp.where(lane == TOP_K + k, rank, ri)
        rg = jnp.where(lane == k, exps[k] / denom, rg)
    ri_ref[...] = ri
    rg_ref[...] = rg
    carry_ref[...] = carry_ref[...] + jnp.sum(hot, axis=0, keepdims=True)
    cnt_ref[...] = carry_ref[...]


def _post(x2d, pp, at, sgp, sga, wbp, wba, wout, g_moe, rw, rb, tm):
    n = x2d.shape[0]
    once = pl.Buffered(1)
    row = lambda w: pl.BlockSpec((tm, w), lambda i: (i, 0))
    full = lambda a, c: pl.BlockSpec((a, c), lambda i: (0, 0), pipeline_mode=once)
    return pl.pallas_call(
        _post_kernel,
        grid=(n // tm,),
        in_specs=[row(D_MODEL), row(POOL_WIDTH), row(ATTN_WIDTH), row(D_MODEL),
                  pl.BlockSpec((tm, D_MODEL), lambda i: (i, 1)),
                  full(POOL_WIDTH, D_MODEL), full(ATTN_WIDTH, D_MODEL), full(D_MODEL, D_MODEL),
                  full(1, D_MODEL), full(D_MODEL, LANES), full(1, LANES)],
        out_specs=[row(D_MODEL), row(D_MODEL), row(LANES), row(LANES),
                   pl.BlockSpec((1, LANES), lambda i: (0, 0))],
        out_shape=[jax.ShapeDtypeStruct((n, D_MODEL), f32), jax.ShapeDtypeStruct((n, D_MODEL), f32),
                   jax.ShapeDtypeStruct((n, LANES), i32), jax.ShapeDtypeStruct((n, LANES), f32),
                   jax.ShapeDtypeStruct((1, LANES), f32)],
        scratch_shapes=[pltpu.VMEM((1, LANES), f32)],
        compiler_params=_cparams(("arbitrary",)),
        name="merge_outproj_router",
    )(x2d, pp, at, sgp, sga, wbp, wba, wout, g_moe, rw, rb)


def _dispatch_kernel(nu_ref, tok_ref, hn_hbm, xs_ref, buf_ref, sem):
    bm = xs_ref.shape[0]

    def row_copy(j):
        return pltpu.make_async_copy(hn_hbm.at[pl.ds(tok_ref[j], 1)], buf_ref.at[pl.ds(j, 1)], sem)

    @pl.when(pl.program_id(0) < nu_ref[0])
    def _():
        def issue(j, carry):
            row_copy(j).start()
            return carry

        def drain(j, carry):
            row_copy(j).wait()
            return carry

        lax.fori_loop(0, bm, issue, 0)
        lax.fori_loop(0, bm, drain, 0)
        xs_ref[...] = buf_ref[...].astype(xs_ref.dtype)

    @pl.when(pl.program_id(0) >= nu_ref[0])
    def _():
        xs_ref[...] = jnp.zeros_like(xs_ref)


def _dispatch(n_used, row_tok, hn, bm):
    n_rows = row_tok.shape[0]
    grid_spec = pltpu.PrefetchScalarGridSpec(
        num_scalar_prefetch=1,
        grid=(n_rows // bm,),
        in_specs=[pl.BlockSpec((bm,), lambda b, nu: (b,), memory_space=pltpu.SMEM),
                  pl.BlockSpec(memory_space=pl.ANY)],
        out_specs=pl.BlockSpec((bm, D_MODEL), lambda b, nu: (b, 0)),
        scratch_shapes=[pltpu.VMEM((bm, D_MODEL), f32), pltpu.SemaphoreType.DMA],
    )
    return pl.pallas_call(
        _dispatch_kernel,
        grid_spec=grid_spec,
        out_shape=jax.ShapeDtypeStruct((n_rows, D_MODEL), bf16),
        compiler_params=_cparams(("arbitrary",)),
        name="moe_dispatch",
    )(n_used, row_tok, hn)


def _expert_kernel(be_ref, nu_ref, xs_ref, wg_ref, wu_ref, wd_ref, bg_ref, bu_ref, bd_ref, y_ref):
    b, f = pl.program_id(0), pl.program_id(1)
    used = b < nu_ref[0]

    @pl.when(used)
    def _():
        x = xs_ref[...]
        gt = jnp.dot(x, wg_ref[...].astype(bf16), preferred_element_type=f32) + bg_ref[...]
        up = jnp.dot(x, wu_ref[...].astype(bf16), preferred_element_type=f32) + bu_ref[...]
        gt = jnp.minimum(gt, SWIGLU_LIMIT)
        up = jnp.clip(up, -SWIGLU_LIMIT, SWIGLU_LIMIT)
        hidden = (up + 1.0) * (gt * jax.nn.sigmoid(SWIGLU_ALPHA * gt))
        part = jnp.dot(hidden.astype(bf16), wd_ref[...].astype(bf16), preferred_element_type=f32)

        @pl.when(f == 0)
        def _():
            y_ref[...] = part + bd_ref[...]

        @pl.when(f > 0)
        def _():
            y_ref[...] += part

    @pl.when(jnp.logical_and(jnp.logical_not(used), f == 0))
    def _():
        y_ref[...] = jnp.zeros_like(y_ref)


def _experts(block_expert, n_used, xs, wg, wu, wd, bg, bu, bd, bm, tf):
    n_rows = xs.shape[0]
    n_blocks = n_rows // bm
    n_f = D_FF // tf

    def blk(b, nu):
        return jnp.minimum(b, nu[0] - 1)

    def ftile(b, f, nu):
        return jnp.where(b < nu[0], f, n_f - 1)

    grid_spec = pltpu.PrefetchScalarGridSpec(
        num_scalar_prefetch=2,
        grid=(n_blocks, n_f),
        in_specs=[pl.BlockSpec((bm, D_MODEL), lambda b, f, be, nu: (blk(b, nu), 0)),
                  pl.BlockSpec((None, D_MODEL, tf), lambda b, f, be, nu: (be[blk(b, nu)], 0, ftile(b, f, nu))),
                  pl.BlockSpec((None, D_MODEL, tf), lambda b, f, be, nu: (be[blk(b, nu)], 0, ftile(b, f, nu))),
                  pl.BlockSpec((None, tf, D_MODEL), lambda b, f, be, nu: (be[blk(b, nu)], ftile(b, f, nu), 0)),
                  pl.BlockSpec((None, 1, tf), lambda b, f, be, nu: (be[blk(b, nu)], 0, ftile(b, f, nu))),
                  pl.BlockSpec((None, 1, tf), lambda b, f, be, nu: (be[blk(b, nu)], 0, ftile(b, f, nu))),
                  pl.BlockSpec((None, 1, D_MODEL), lambda b, f, be, nu: (be[blk(b, nu)], 0, 0))],
        out_specs=pl.BlockSpec((bm, D_MODEL), lambda b, f, be, nu: (b, 0)),
    )
    return pl.pallas_call(
        _expert_kernel,
        grid_spec=grid_spec,
        out_shape=jax.ShapeDtypeStruct((n_rows, D_MODEL), f32),
        compiler_params=_cparams(("arbitrary", "arbitrary")),
        name="moe_experts",
    )(block_expert, n_used, xs, wg, wu, wd, bg, bu, bd)


def _combine_kernel(dest_ref, y_hbm, rg_ref, h1_ref, g_ref, o_ref, buf_ref, sem, *, tm):
    def row_copy(j, k):
        return pltpu.make_async_copy(y_hbm.at[pl.ds(dest_ref[j * TOP_K + k], 1)],
                                     buf_ref.at[k, pl.ds(j, 1)], sem)

    def issue(j, carry):
        for k in range(TOP_K):
            row_copy(j, k).start()
        return carry

    def drain(j, carry):
        for k in range(TOP_K):
            row_copy(j, k).wait()
        return carry

    lax.fori_loop(0, tm, issue, 0)
    lax.fori_loop(0, tm, drain, 0)
    rg = rg_ref[...]
    h = h1_ref[...]
    for k in range(TOP_K):
        h = h + buf_ref[k] * rg[:, k:k + 1]
    o_ref[...] = h * lax.rsqrt(jnp.mean(h * h, axis=-1, keepdims=True) + EPS) * g_ref[...]


def _combine(dest_flat, y, rg, h1, g_fin, tm):
    n = h1.shape[0]
    return pl.pallas_call(
        functools.partial(_combine_kernel, tm=tm),
        grid=(n // tm,),
        in_specs=[pl.BlockSpec((tm * TOP_K,), lambda i: (i,), memory_space=pltpu.SMEM),
                  pl.BlockSpec(memory_space=pl.ANY),
                  pl.BlockSpec((tm, LANES), lambda i: (i, 0)),
                  pl.BlockSpec((tm, D_MODEL), lambda i: (i, 0)),
                  pl.BlockSpec((1, D_MODEL), lambda i: (0, 0))],
        out_specs=pl.BlockSpec((tm, D_MODEL), lambda i: (i, 0)),
        out_shape=jax.ShapeDtypeStruct((n, D_MODEL), f32),
        scratch_shapes=[pltpu.VMEM((TOP_K, tm, D_MODEL), f32), pltpu.SemaphoreType.DMA],
        compiler_params=_cparams(("arbitrary",)),
        name="moe_combine_norm",
    )(dest_flat, y, rg, h1, g_fin)


def _rope_tables(n_pos):
    freqs = ROPE_THETA ** (-jnp.arange(0, ROT_DIM, 2, dtype=f32) / ROT_DIM)
    ang = jnp.arange(n_pos, dtype=f32)[:, None] * freqs[None, :]
    cos, sin = jnp.cos(ang), jnp.sin(ang)
    ones = jnp.ones((n_pos, HEAD_DIM - ROT_DIM), f32)
    zeros8 = jnp.zeros((n_pos, ROT_HALF), f32)
    zeros48 = jnp.zeros((n_pos, HEAD_DIM - ROT_DIM), f32)
    c = jnp.concatenate([cos, cos, ones], axis=1)
    s1 = jnp.concatenate([zeros8, sin, zeros48], axis=1)
    s2 = jnp.concatenate([-sin, zeros8, zeros48], axis=1)
    tile2 = lambda t: jnp.concatenate([t, t], axis=1)
    return (tile2(c), tile2(s1), tile2(s2)), (cos.T, sin.T)


def kernel(x, meta_tokens, norm_mix_g, w_in, pool_w, pool_scale, w_branch_pool, w_branch_attn, w_out,
           norm_moe_g, router_w, router_b, exp_w_gate, exp_b_gate, exp_w_up, exp_b_up, exp_w_down,
           exp_b_down, norm_final_g):
    batch, seq, _ = x.shape
    n_tok = batch * seq
    n_sel = min(TOPK_KEYS_MAX, (seq + N_META) // 4)
    x2d = x.reshape(n_tok, D_MODEL)
    qb = ATTN_QB

    w = w_in[0]
    o_u, o_q, o_k, o_v, o_iq = 0, POOL_WIDTH, POOL_WIDTH + ATTN_WIDTH, POOL_WIDTH + 2 * ATTN_WIDTH, POOL_WIDTH + 3 * ATTN_WIDTH
    o_ik = o_iq + N_IDX_HEADS * IDX_DIM
    o_iw = o_ik + IDX_DIM
    o_gp = o_iw + N_IDX_HEADS
    o_ga = o_gp + D_MODEL
    cast = lambda a: a.astype(bf16)
    w_u, w_q, w_v, w_iq = (cast(w[:, o:o + 1024]) for o in (o_u, o_q, o_v, o_iq))
    w_kT = cast(w[:, o_k:o_k + ATTN_WIDTH].T)
    w_ikT = cast(w[:, o_ik:o_ik + IDX_DIM].T)
    w_iw = cast(jnp.pad(w[:, o_iw:o_iw + N_IDX_HEADS], ((0, 0), (0, LANES - N_IDX_HEADS)))
                * (N_IDX_HEADS ** -0.5 * IDX_DIM ** -0.5))
    w_gates = cast(w[:, o_gp:o_ga + D_MODEL])

    (tab_tok, (cosT, sinT)) = _rope_tables(seq + N_META)
    tok_x = tuple(t[N_META:] for t in tab_tok)
    cosT_x, sinT_x = cosT[:, N_META:], sinT[:, N_META:]
    pad_cols = lambda t, n: jnp.pad(t[:, :n] if t.shape[1] >= n else t, ((0, 0), (0, max(0, n - t.shape[1]))))
    cosT_m, sinT_m = pad_cols(cosT, qb), pad_cols(sinT, qb)

    xn = _rmsnorm(x2d, norm_mix_g[0], NORM_TM)
    meta_pad = jnp.pad(meta_tokens.astype(f32), ((0, qb - N_META), (0, 0)))
    xn_meta = _rmsnorm(meta_pad, norm_mix_g[0], qb)

    u = _proj(xn, w_u, PROJ_TM, f32)
    v = _proj(xn, w_v, PROJ_TM, bf16)
    sg = _proj(xn, w_gates, PROJ_TM, bf16, act="sigmoid")
    iw = _proj(xn, w_iw, PROJ_TM, f32)
    q = _proj_rope(xn, w_q, tok_x, PROJ_TM, seq // PROJ_TM, HEAD_DIM ** -0.5)
    iq = _proj_rope(xn, w_iq, tok_x, PROJ_TM, seq // PROJ_TM, 1.0)
    kT = _projT_rope(xn, w_kT, cosT_x, sinT_x, qb, seq // qb, N_HEADS)
    ikT = _projT_rope(xn, w_ikT, cosT_x, sinT_x, qb, seq // qb, 1)

    u_meta = _proj(xn_meta, w_u, qb, f32)[:N_META]
    v_meta = _proj(xn_meta, w_v, qb, bf16)
    kT_meta = _projT_rope(xn_meta, w_kT, cosT_m, sinT_m, qb, 1, N_HEADS)[0]
    ikT_meta = _projT_rope(xn_meta, w_ikT, cosT_m, sinT_m, qb, 1, 1)[0]

    pp = _pool(u.reshape(batch, seq, POOL_WIDTH), u_meta, cast(pool_w[0]), pool_scale[0].reshape(1, POOL_WIDTH),
               POOL_TM).reshape(n_tok, POOL_WIDTH)
    at = _attention(q, iq, iw, kT.reshape(batch, seq // qb, ATTN_WIDTH, qb), v,
                    ikT.reshape(batch, seq // qb, IDX_DIM, qb), kT_meta, v_meta, ikT_meta, batch, seq, n_sel)

    rw = jnp.pad(router_w[0].astype(f32), ((0, 0), (0, LANES - N_EXPERTS)))
    rb = jnp.pad(router_b[0].astype(f32), (0, LANES - N_EXPERTS)).reshape(1, LANES)
    h1, hn, ri, rg, cnt = _post(x2d, pp, at, sg, sg, cast(w_branch_pool[0]),
                                cast(w_branch_attn[0]), cast(w_out[0]), norm_moe_g[0].reshape(1, D_MODEL),
                                rw, rb, POST_TM)

    bm = MOE_BM
    counts = cnt[0, :N_EXPERTS].astype(i32)
    padded = ((counts + bm - 1) // bm) * bm
    pad_end = jnp.cumsum(padded)
    pad_start = pad_end - padded
    n_blocks = -(-(n_tok * TOP_K + N_EXPERTS * (bm - 1)) // bm)
    dest = (pad_start[ri[:, :TOP_K]] + ri[:, TOP_K:2 * TOP_K]).reshape(n_tok * TOP_K)
    block_expert = jnp.minimum(jnp.searchsorted(pad_end, jnp.arange(n_blocks, dtype=i32) * bm, side="right"),
                               N_EXPERTS - 1).astype(i32)
    n_used = (pad_end[-1] // bm).astype(i32).reshape(1)

    row_tok = jnp.zeros((n_blocks * bm,), i32).at[dest].set(jnp.arange(n_tok * TOP_K, dtype=i32) // TOP_K)
    xs = _dispatch(n_used, row_tok, hn, bm)
    y = _experts(block_expert, n_used, xs, exp_w_gate[0], exp_w_up[0], exp_w_down[0],
                 exp_b_gate[0].reshape(N_EXPERTS, 1, D_FF), exp_b_up[0].reshape(N_EXPERTS, 1, D_FF),
                 exp_b_down[0].reshape(N_EXPERTS, 1, D_MODEL), bm, MOE_TF)
    out = _combine(dest, y, rg, h1, norm_final_g.reshape(1, D_MODEL), COMBINE_TM)
    return out.reshape(batch, seq, D_MODEL)
```

```python
import functools

import jax
import jax.numpy as jnp
from jax import lax
from jax.experimental import pallas as pl
from jax.experimental.pallas import tpu as pltpu

f32 = jnp.float32
bf16 = jnp.bfloat16
i32 = jnp.int32

D_MODEL = 2048
N_META = 16
POOL_WINDOWS = (2, 4, 8, 16)
N_POOL_GROUPS = 4
POOL_WIDTH = D_MODEL // 2
POOL_GROUP_DIM = POOL_WIDTH // N_POOL_GROUPS
N_HEADS = 16
HEAD_DIM = 64
ATTN_WIDTH = N_HEADS * HEAD_DIM
ROT_DIM = HEAD_DIM // 4
ROT_HALF = ROT_DIM // 2
ROPE_THETA = 500000.0
N_IDX_HEADS = 16
IDX_DIM = 64
TOPK_KEYS_MAX = 256
N_EXPERTS = 32
TOP_K = 4
D_FF = D_MODEL
SWIGLU_LIMIT = 7.0
SWIGLU_ALPHA = 1.702
EPS = 1e-5

LANES = 128
VMEM_LIMIT = 56 * 1024 * 1024

NORM_TM = 512
PROJ_TM = 512
ATTN_QB = 256
ATTN_KC = 512
ATTN_MC = 128
ATTN_GROUP = 4
POOL_TM = 512
POST_TM = 256
MOE_BM = 1024
MOE_TF = 256
COMBINE_TM = 256

NEG_BIG = -1e30
INT_MIN = -(2 ** 31)


def _cparams(sem):
    return pltpu.CompilerParams(dimension_semantics=sem, vmem_limit_bytes=VMEM_LIMIT)


def _rmsnorm_kernel(x_ref, g_ref, o_ref):
    x = x_ref[...]
    ms = jnp.mean(x * x, axis=-1, keepdims=True)
    o_ref[...] = (x * lax.rsqrt(ms + EPS) * g_ref[...]).astype(o_ref.dtype)


def _rmsnorm(x2d, g, tm):
    n = x2d.shape[0]
    return pl.pallas_call(
        _rmsnorm_kernel,
        grid=(n // tm,),
        in_specs=[pl.BlockSpec((tm, D_MODEL), lambda i: (i, 0)),
                  pl.BlockSpec((1, D_MODEL), lambda i: (0, 0))],
        out_specs=pl.BlockSpec((tm, D_MODEL), lambda i: (i, 0)),
        out_shape=jax.ShapeDtypeStruct((n, D_MODEL), bf16),
        compiler_params=_cparams(("parallel",)),
        name="rmsnorm",
    )(x2d, g.reshape(1, D_MODEL))


def _proj_kernel(xn_ref, w_ref, o_ref, *, act):
    y = jnp.dot(xn_ref[...], w_ref[...], preferred_element_type=f32)
    if act == "sigmoid":
        y = jax.nn.sigmoid(y)
    o_ref[...] = y.astype(o_ref.dtype)


def _proj(xn, w, tm, out_dtype, act=None, tn=1024):
    n, width = xn.shape[0], w.shape[1]
    tn = min(tn, width)
    return pl.pallas_call(
        functools.partial(_proj_kernel, act=act),
        grid=(width // tn, n // tm),
        in_specs=[pl.BlockSpec((tm, D_MODEL), lambda j, i: (i, 0)),
                  pl.BlockSpec((D_MODEL, tn), lambda j, i: (0, j))],
        out_specs=pl.BlockSpec((tm, tn), lambda j, i: (i, j)),
        out_shape=jax.ShapeDtypeStruct((n, width), out_dtype),
        compiler_params=_cparams(("parallel", "parallel")),
        name="proj_" + (act or "plain"),
    )(xn, w)


def _proj_rope_kernel(xn_ref, w_ref, c_ref, s1_ref, s2_ref, o_ref, *, scale):
    y = jnp.dot(xn_ref[...], w_ref[...], preferred_element_type=f32)
    c, s1, s2 = c_ref[...], s1_ref[...], s2_ref[...]
    for j in range(y.shape[1] // LANES):
        yt = y[:, j * LANES:(j + 1) * LANES]
        r = yt * c + pltpu.roll(yt, ROT_HALF, 1) * s1 + pltpu.roll(yt, LANES - ROT_HALF, 1) * s2
        o_ref[:, j * LANES:(j + 1) * LANES] = (r * scale).astype(o_ref.dtype)


def _proj_rope(xn, w, tables, tm, pos_blocks, scale):
    n, width = xn.shape[0], w.shape[1]
    tspec = pl.BlockSpec((tm, LANES), lambda i: (i % pos_blocks, 0))
    return pl.pallas_call(
        functools.partial(_proj_rope_kernel, scale=scale),
        grid=(n // tm,),
        in_specs=[pl.BlockSpec((tm, D_MODEL), lambda i: (i, 0)),
                  pl.BlockSpec((D_MODEL, width), lambda i: (0, 0)),
                  tspec, tspec, tspec],
        out_specs=pl.BlockSpec((tm, width), lambda i: (i, 0)),
        out_shape=jax.ShapeDtypeStruct((n, width), bf16),
        compiler_params=_cparams(("parallel",)),
        name="proj_rope",
    )(xn, w, *tables)


def _projT_rope_kernel(wT_ref, xn_ref, cos_ref, sin_ref, o_ref, *, n_heads):
    y = lax.dot_general(wT_ref[...], xn_ref[...], (((1,), (1,)), ((), ())),
                        preferred_element_type=f32)
    c, s = cos_ref[...], sin_ref[...]
    for h in range(n_heads):
        blk = y[h * HEAD_DIM:(h + 1) * HEAD_DIM]
        x1, x2 = blk[0:ROT_HALF], blk[ROT_HALF:ROT_DIM]
        out = jnp.concatenate([x1 * c - x2 * s, x2 * c + x1 * s, blk[ROT_DIM:]], axis=0)
        o_ref[h * HEAD_DIM:(h + 1) * HEAD_DIM, :] = out.astype(o_ref.dtype)


def _projT_rope(xn, wT, cosT, sinT, tm, pos_blocks, n_heads):
    n, rows = xn.shape[0], wT.shape[0]
    tspec = pl.BlockSpec((ROT_HALF, tm), lambda i: (0, i % pos_blocks))
    return pl.pallas_call(
        functools.partial(_projT_rope_kernel, n_heads=n_heads),
        grid=(n // tm,),
        in_specs=[pl.BlockSpec((rows, D_MODEL), lambda i: (0, 0)),
                  pl.BlockSpec((tm, D_MODEL), lambda i: (i, 0)),
                  tspec, tspec],
        out_specs=pl.BlockSpec((None, rows, tm), lambda i: (i, 0, 0)),
        out_shape=jax.ShapeDtypeStruct((n // tm, rows, tm), bf16),
        compiler_params=_cparams(("parallel",)),
        name="projT_rope",
    )(wT, xn, cosT, sinT)


def _pool_kernel(u_ref, prev_ref, meta_ref, pw_ref, ps_ref, o_ref, ext_ref, *, tm):
    i = pl.program_id(1)
    halo = jnp.where(i == 0, meta_ref[...], prev_ref[...])
    ext_ref[0:N_META, :] = halo
    ext_ref[N_META:, :] = u_ref[...]
    pos = i * tm + lax.broadcasted_iota(i32, (tm, 1), 0) + N_META
    for g, w in enumerate(POOL_WINDOWS):
        cols = slice(g * POOL_GROUP_DIM, (g + 1) * POOL_GROUP_DIM)
        acc = ext_ref[N_META:, cols]
        for j in range(1, w):
            acc = acc + ext_ref[pl.ds(N_META - j, tm), cols]
        count = jnp.minimum(pos + 1, w).astype(f32)
        pooled = acc / count - ext_ref[N_META:, cols]
        mixed = jnp.dot(pooled.astype(bf16), pw_ref[g], preferred_element_type=f32)
        o_ref[:, cols] = (mixed * ps_ref[:, cols]).astype(o_ref.dtype)


def _pool(u3, u_meta, pool_w, pool_scale, tm):
    b, s, _ = u3.shape
    per = tm // N_META
    return pl.pallas_call(
        functools.partial(_pool_kernel, tm=tm),
        grid=(b, s // tm),
        in_specs=[pl.BlockSpec((None, tm, POOL_WIDTH), lambda bb, i: (bb, i, 0)),
                  pl.BlockSpec((None, N_META, POOL_WIDTH),
                               lambda bb, i: (bb, jnp.maximum(i * per - 1, 0), 0)),
                  pl.BlockSpec((N_META, POOL_WIDTH), lambda bb, i: (0, 0)),
                  pl.BlockSpec((N_POOL_GROUPS, POOL_GROUP_DIM, POOL_GROUP_DIM), lambda bb, i: (0, 0, 0)),
                  pl.BlockSpec((1, POOL_WIDTH), lambda bb, i: (0, 0))],
        out_specs=pl.BlockSpec((None, tm, POOL_WIDTH), lambda bb, i: (bb, i, 0)),
        out_shape=jax.ShapeDtypeStruct((b, s, POOL_WIDTH), bf16),
        scratch_shapes=[pltpu.VMEM((tm + N_META, POOL_WIDTH), f32)],
        compiler_params=_cparams(("parallel", "parallel")),
        name="pool_mixer",
    )(u3, u3, u_meta, pool_w, pool_scale)


def _order_key(score):
    bits = pltpu.bitcast(score + 0.0, i32)
    return jnp.where(bits < 0, bits ^ jnp.int32(0x7FFFFFFF), bits)


def _attn_kernel(q_ref, iq_ref, iw_ref, kT_ref, v_ref, ikT_ref, kTm_ref, vm_ref, ikTm_ref,
                 o_ref, key_ref, keym_ref, iwb_ref, s_ref, sm_ref, mrun_ref, lrun_ref, acc_ref,
                 *, n_sel, n_chunks, ord_bits):
    qb, kc, mc = q_ref.shape[0], kT_ref.shape[2], kTm_ref.shape[1]
    half = qb // 2
    sub = 2 * LANES
    meta = n_chunks
    qi = pl.program_id(1)
    n_ch = (qi * qb + qb - 1) // kc + 1
    row = qi * qb + lax.broadcasted_iota(i32, (qb, 1), 0)

    def tile_lanes(a, width):
        return jnp.concatenate([a] * (width // LANES), axis=1) if width > LANES else a

    def lane_max(a):
        out = a[:, :LANES]
        for j in range(1, a.shape[1] // LANES):
            out = jnp.maximum(out, a[:, j * LANES:(j + 1) * LANES])
        return out

    def lane_sum(a):
        out = a[:, :LANES]
        for j in range(1, a.shape[1] // LANES):
            out = out + a[:, j * LANES:(j + 1) * LANES]
        return out

    iq_heads = [iq_ref[:, h * IDX_DIM:(h + 1) * IDX_DIM] for h in range(N_IDX_HEADS)]
    iw = iw_ref[...]
    for h in range(N_IDX_HEADS):
        iwb_ref[h] = jnp.broadcast_to(iw[:, h:h + 1], (qb, LANES))

    def index_keys(ikT_c, valid):
        width = ikT_c.shape[1]
        acc = jnp.zeros((qb, width), f32)
        for h in range(N_IDX_HEADS):
            lg = jnp.dot(iq_heads[h], ikT_c, preferred_element_type=f32)
            acc = acc + jnp.maximum(lg, 0.0) * tile_lanes(iwb_ref[h], width)
        return jnp.where(valid, _order_key(acc), INT_MIN)

    def score_body(c, carry):
        for j in range(kc // sub):
            col = c * kc + j * sub + lax.broadcasted_iota(i32, (1, sub), 1)
            key_ref[c, :, j * sub:(j + 1) * sub] = index_keys(ikT_ref[c, :, j * sub:(j + 1) * sub], col <= row)
        return carry

    lax.fori_loop(0, n_ch, score_body, 0)
    keym_ref[...] = index_keys(ikTm_ref[...], lax.broadcasted_iota(i32, (1, mc), 1) < N_META)

    def keys_of(c, rows=slice(None)):
        return keym_ref[rows, :] if isinstance(c, int) and c == meta else key_ref[c, rows, :]

    def store_bias(c, bias):
        if isinstance(c, int) and c == meta:
            keym_ref[...] = pltpu.bitcast(bias, i32)
        else:
            key_ref[c] = pltpu.bitcast(bias, i32)

    def count_rows(pred_fn, operands):
        halves = [slice(r * half, (r + 1) * half) for r in range(2)]
        bcast = [[jnp.broadcast_to(o[rows], (half, LANES)) for o in operands] for rows in halves]
        accs = []
        for rows, ops in zip(halves, bcast):
            def add(c, acc, rows=rows, ops=ops):
                k = keys_of(c, rows)
                for j in range(k.shape[1] // LANES):
                    hit = pred_fn(c, j, k[:, j * LANES:(j + 1) * LANES], *ops)
                    acc = acc + jnp.where(hit, 1.0, 0.0)
                return acc
            accs.append(lax.fori_loop(0, n_ch, add, add(meta, jnp.zeros((half, LANES), f32))))
        return jnp.sum(jnp.concatenate(accs, axis=0), axis=1, keepdims=True)

    def count_ge(cand):
        return count_rows(lambda c, j, k, cb: k >= cb, [cand])

    t0 = jnp.where(count_ge(jnp.zeros((qb, 1), i32)) >= n_sel, 0, INT_MIN).astype(i32)

    def bisect(it, t):
        cand = t + lax.shift_left(jnp.int32(1), 30 - it)
        return jnp.where(count_ge(cand) >= n_sel, cand, t)

    thr = lax.fori_loop(0, 31, bisect, t0)
    n_ge = count_ge(thr)
    tied = jnp.logical_and(n_ge > n_sel, thr > INT_MIN)
    any_tied = jnp.max(jnp.where(tied, 1.0, 0.0)) > 0.0
    floor_key = jnp.maximum(thr, INT_MIN + 1)

    def ordinal(c, lanes):
        return jnp.where(c == meta, lanes, c * kc + lanes + N_META)

    @pl.when(jnp.logical_not(any_tied))
    def _():
        def to_bias(c, carry):
            store_bias(c, jnp.where(keys_of(c) >= floor_key, 0.0, NEG_BIG).astype(f32))
            return carry
        lax.fori_loop(0, n_ch, to_bias, 0)
        to_bias(meta, 0)

    @pl.when(any_tied)
    def _():
        need = n_sel - count_ge(thr + 1)
        lane_t = lax.broadcasted_iota(i32, (1, LANES), 1)

        def tie_pred(c, j, k, tb, cb):
            return jnp.logical_and(k == tb, ordinal(c, lane_t + j * LANES) <= cb)

        def bisect_pos(it, lo):
            cand = lo + lax.shift_left(jnp.int32(1), ord_bits - 1 - it)
            cnt = count_rows(tie_pred, [thr, cand])
            return jnp.where(cnt < need, cand, lo)

        lo = lax.fori_loop(0, ord_bits, bisect_pos, jnp.full((qb, 1), -1, i32))
        last = jnp.where(tied, lo + 1, jnp.int32(2 ** 30))

        def to_bias(c, carry):
            k = keys_of(c)
            lanes = lax.broadcasted_iota(i32, (1, k.shape[1]), 1)
            keep = jnp.logical_and(k >= floor_key, jnp.logical_or(k != thr, ordinal(c, lanes) <= last))
            store_bias(c, jnp.where(keep, 0.0, NEG_BIG).astype(f32))
            return carry
        lax.fori_loop(0, n_ch, to_bias, 0)
        to_bias(meta, 0)

    n_grp = s_ref.shape[0]
    low_half = lax.broadcasted_iota(i32, (1, 2 * HEAD_DIM), 1) < HEAD_DIM
    for grp in range(N_HEADS // n_grp):
        heads = [grp * n_grp + g for g in range(n_grp)]
        q_heads = [q_ref[:, h * HEAD_DIM:(h + 1) * HEAD_DIM] for h in heads]
        mrun_ref[...] = jnp.full(mrun_ref.shape, NEG_BIG, f32)

        def sweep_scores(c, carry):
            bias = pltpu.bitcast(key_ref[c], f32)
            for g, h in enumerate(heads):
                s = jnp.dot(q_heads[g], kT_ref[c, h * HEAD_DIM:(h + 1) * HEAD_DIM, :],
                            preferred_element_type=f32) + bias
                s_ref[g, c] = s
                mrun_ref[g] = jnp.maximum(mrun_ref[g], lane_max(s))
            return carry

        lax.fori_loop(0, n_ch, sweep_scores, 0)
        bias_m = pltpu.bitcast(keym_ref[...], f32)
        for g, h in enumerate(heads):
            s = jnp.dot(q_heads[g], kTm_ref[h * HEAD_DIM:(h + 1) * HEAD_DIM, :], preferred_element_type=f32) + bias_m
            sm_ref[g] = s
            row_max = jnp.max(jnp.maximum(mrun_ref[g], lane_max(s)), axis=1, keepdims=True)
            mrun_ref[g] = jnp.broadcast_to(row_max, (qb, LANES))

        lrun_ref[...] = jnp.zeros(lrun_ref.shape, f32)
        acc_ref[...] = jnp.zeros(acc_ref.shape, f32)

        def accumulate(get_s, v_rows, width):
            for hp in range(n_grp // 2):
                lanes = slice((heads[0] + 2 * hp) * HEAD_DIM, (heads[0] + 2 * hp + 2) * HEAD_DIM)
                v_pair = v_rows[:, lanes]
                parts = []
                for g in (2 * hp, 2 * hp + 1):
                    p = jnp.exp(get_s(g) - tile_lanes(mrun_ref[g], width))
                    lrun_ref[g] = lrun_ref[g] + lane_sum(p)
                    parts.append(jnp.dot(p.astype(bf16), v_pair, preferred_element_type=f32))
                out = slice(2 * hp * HEAD_DIM, (2 * hp + 2) * HEAD_DIM)
                acc_ref[:, out] = acc_ref[:, out] + jnp.where(low_half, parts[0], parts[1])

        def sweep_values(c, carry):
            start = pl.multiple_of(c * kc, kc)
            accumulate(lambda g: s_ref[g, c], v_ref.at[pl.ds(start, kc)], kc)
            return carry

        lax.fori_loop(0, n_ch, sweep_values, 0)
        accumulate(lambda g: sm_ref[g], vm_ref, mc)
        for hp in range(n_grp // 2):
            out = slice(2 * hp * HEAD_DIM, (2 * hp + 2) * HEAD_DIM)
            denom = jnp.where(low_half, jnp.sum(lrun_ref[2 * hp], axis=1, keepdims=True),
                              jnp.sum(lrun_ref[2 * hp + 1], axis=1, keepdims=True))
            lanes = slice((heads[0] + 2 * hp) * HEAD_DIM, (heads[0] + 2 * hp + 2) * HEAD_DIM)
            o_ref[:, lanes] = (acc_ref[:, out] / denom).astype(o_ref.dtype)


def _attention(q, iq, iw, kT, v, ikT, kT_meta, v_meta, ikT_meta, batch, seq, n_sel):
    qb, kc, mc, grp = ATTN_QB, ATTN_KC, ATTN_MC, ATTN_GROUP
    n_q = seq // qb
    n_chunks = seq // kc
    ord_bits = max(1, (seq + N_META - 1).bit_length())
    kernel = functools.partial(_attn_kernel, n_sel=n_sel, n_chunks=n_chunks, ord_bits=ord_bits)
    once = pl.Buffered(1)
    return pl.pallas_call(
        kernel,
        grid=(batch, n_q),
        in_specs=[pl.BlockSpec((qb, ATTN_WIDTH), lambda b, i: (b * n_q + i, 0)),
                  pl.BlockSpec((qb, N_IDX_HEADS * IDX_DIM), lambda b, i: (b * n_q + i, 0)),
                  pl.BlockSpec((qb, LANES), lambda b, i: (b * n_q + i, 0)),
                  pl.BlockSpec((None, n_chunks, ATTN_WIDTH, kc), lambda b, i: (b, 0, 0, 0), pipeline_mode=once),
                  pl.BlockSpec((seq, ATTN_WIDTH), lambda b, i: (b, 0), pipeline_mode=once),
                  pl.BlockSpec((None, n_chunks, IDX_DIM, kc), lambda b, i: (b, 0, 0, 0), pipeline_mode=once),
                  pl.BlockSpec((ATTN_WIDTH, mc), lambda b, i: (0, 0)),
                  pl.BlockSpec((mc, ATTN_WIDTH), lambda b, i: (0, 0)),
                  pl.BlockSpec((IDX_DIM, mc), lambda b, i: (0, 0))],
        out_specs=pl.BlockSpec((qb, ATTN_WIDTH), lambda b, i: (b * n_q + i, 0)),
        out_shape=jax.ShapeDtypeStruct((batch * seq, ATTN_WIDTH), bf16),
        scratch_shapes=[pltpu.VMEM((n_chunks, qb, kc), i32),
                        pltpu.VMEM((qb, mc), i32),
                        pltpu.VMEM((N_IDX_HEADS, qb, LANES), f32),
                        pltpu.VMEM((grp, n_chunks, qb, kc), f32),
                        pltpu.VMEM((grp, qb, mc), f32),
                        pltpu.VMEM((grp, qb, LANES), f32),
                        pltpu.VMEM((grp, qb, LANES), f32),
                        pltpu.VMEM((qb, grp * HEAD_DIM), f32)],
        compiler_params=_cparams(("parallel", "arbitrary")),
        name="indexed_attention",
    )(q, iq, iw, kT, v, ikT, kT_meta, v_meta, ikT_meta)


def _post_kernel(x_ref, pp_ref, at_ref, sgp_ref, sga_ref, wbp_ref, wba_ref, wout_ref, g_ref, rw_ref, rb_ref,
                 h1_ref, hn_ref, ri_ref, rg_ref, cnt_ref, carry_ref):
    tm = x_ref.shape[0]

    @pl.when(pl.program_id(0) == 0)
    def _():
        carry_ref[...] = jnp.zeros_like(carry_ref)

    y_pool = jnp.dot(pp_ref[...], wbp_ref[...], preferred_element_type=f32)
    y_attn = jnp.dot(at_ref[...], wba_ref[...], preferred_element_type=f32)
    merged = sgp_ref[...].astype(f32) * y_pool + sga_ref[...].astype(f32) * y_attn
    h1 = x_ref[...] + jnp.dot(merged.astype(bf16), wout_ref[...], preferred_element_type=f32)
    h1_ref[...] = h1
    hn = h1 * lax.rsqrt(jnp.mean(h1 * h1, axis=-1, keepdims=True) + EPS) * g_ref[...]
    hn_ref[...] = hn

    lane = lax.broadcasted_iota(i32, (tm, LANES), 1)
    logits = jnp.dot(hn, rw_ref[...], preferred_element_type=f32, precision=lax.Precision.HIGHEST) + rb_ref[...]
    logits = jnp.where(lane < N_EXPERTS, logits, -jnp.inf)
    vals, idxs = [], []
    for _ in range(TOP_K):
        m = jnp.max(logits, axis=1, keepdims=True)
        idx = jnp.min(jnp.where(logits == m, lane, LANES), axis=1, keepdims=True)
        vals.append(m)
        idxs.append(idx)
        logits = jnp.where(lane == idx, -jnp.inf, logits)
    exps = [jnp.exp(v - vals[0]) for v in vals]
    denom = exps[0] + exps[1] + exps[2] + exps[3]

    hot = jnp.zeros((tm, LANES), f32)
    for idx in idxs:
        hot = hot + jnp.where(lane == idx, 1.0, 0.0)
    r_i = lax.broadcasted_iota(i32, (tm, tm), 0)
    c_i = lax.broadcasted_iota(i32, (tm, tm), 1)
    below = jnp.where(c_i < r_i, 1.0, 0.0).astype(bf16)
    before = jnp.dot(below, hot.astype(bf16), preferred_element_type=f32) + carry_ref[...]
    ri = jnp.zeros((tm, LANES), i32)
    rg = jnp.zeros((tm, LANES), f32)
    for k in range(TOP_K):
        rank = jnp.sum(jnp.where(lane == idxs[k], before, 0.0), axis=1, keepdims=True).astype(i32)
        ri = jnp.where(lane == k, idxs[k], ri)
        ri = jnp.where(lane == TOP_K + k, rank, ri)
        rg = jnp.where(lane == k, exps[k] / denom, rg)
    ri_ref[...] = ri
    rg_ref[...] = rg
    carry_ref[...] = carry_ref[...] + jnp.sum(hot, axis=0, keepdims=True)
    cnt_ref[...] = carry_ref[...]


def _post(x2d, pp, at, sgp, sga, wbp, wba, wout, g_moe, rw, rb, tm):
    n = x2d.shape[0]
    once = pl.Buffered(1)
    row = lambda w: pl.BlockSpec((tm, w), lambda i: (i, 0))
    full = lambda a, c: pl.BlockSpec((a, c), lambda i: (0, 0), pipeline_mode=once)
    return pl.pallas_call(
        _post_kernel,
        grid=(n // tm,),
        in_specs=[row(D_MODEL), row(POOL_WIDTH), row(ATTN_WIDTH), row(D_MODEL),
                  pl.BlockSpec((tm, D_MODEL), lambda i: (i, 1)),
                  full(POOL_WIDTH, D_MODEL), full(ATTN_WIDTH, D_MODEL), full(D_MODEL, D_MODEL),
                  full(1, D_MODEL), full(D_MODEL, LANES), full(1, LANES)],
        out_specs=[row(D_MODEL), row(D_MODEL), row(LANES), row(LANES),
                   pl.BlockSpec((1, LANES), lambda i: (0, 0))],
        out_shape=[jax.ShapeDtypeStruct((n, D_MODEL), f32), jax.ShapeDtypeStruct((n, D_MODEL), f32),
                   jax.ShapeDtypeStruct((n, LANES), i32), jax.ShapeDtypeStruct((n, LANES), f32),
                   jax.ShapeDtypeStruct((1, LANES), f32)],
        scratch_shapes=[pltpu.VMEM((1, LANES), f32)],
        compiler_params=_cparams(("arbitrary",)),
        name="merge_outproj_router",
    )(x2d, pp, at, sgp, sga, wbp, wba, wout, g_moe, rw, rb)


def _dispatch_kernel(nu_ref, tok_ref, hn_hbm, xs_ref, buf_ref, sem):
    bm = xs_ref.shape[0]

    def row_copy(j):
        return pltpu.make_async_copy(hn_hbm.at[pl.ds(tok_ref[j], 1)], buf_ref.at[pl.ds(j, 1)], sem)

    @pl.when(pl.program_id(0) < nu_ref[0])
    def _():
        def issue(j, carry):
            row_copy(j).start()
            return carry

        def drain(j, carry):
            row_copy(j).wait()
            return carry

        lax.fori_loop(0, bm, issue, 0)
        lax.fori_loop(0, bm, drain, 0)
        xs_ref[...] = buf_ref[...].astype(xs_ref.dtype)

    @pl.when(pl.program_id(0) >= nu_ref[0])
    def _():
        xs_ref[...] = jnp.zeros_like(xs_ref)


def _dispatch(n_used, row_tok, hn, bm):
    n_rows = row_tok.shape[0]
    grid_spec = pltpu.PrefetchScalarGridSpec(
        num_scalar_prefetch=1,
        grid=(n_rows // bm,),
        in_specs=[pl.BlockSpec((bm,), lambda b, nu: (b,), memory_space=pltpu.SMEM),
                  pl.BlockSpec(memory_space=pl.ANY)],
        out_specs=pl.BlockSpec((bm, D_MODEL), lambda b, nu: (b, 0)),
        scratch_shapes=[pltpu.VMEM((bm, D_MODEL), f32), pltpu.SemaphoreType.DMA],
    )
    return pl.pallas_call(
        _dispatch_kernel,
        grid_spec=grid_spec,
        out_shape=jax.ShapeDtypeStruct((n_rows, D_MODEL), bf16),
        compiler_params=_cparams(("arbitrary",)),
        name="moe_dispatch",
    )(n_used, row_tok, hn)


def _expert_kernel(be_ref, nu_ref, xs_ref, wg_ref, wu_ref, wd_ref, bg_ref, bu_ref, bd_ref, y_ref):
    b, f = pl.program_id(0), pl.program_id(1)
    used = b < nu_ref[0]

    @pl.when(used)
    def _():
        x = xs_ref[...]
        gt = jnp.dot(x, wg_ref[...].astype(bf16), preferred_element_type=f32) + bg_ref[...]
        up = jnp.dot(x, wu_ref[...].astype(bf16), preferred_element_type=f32) + bu_ref[...]
        gt = jnp.minimum(gt, SWIGLU_LIMIT)
        up = jnp.clip(up, -SWIGLU_LIMIT, SWIGLU_LIMIT)
        hidden = (up + 1.0) * (gt * jax.nn.sigmoid(SWIGLU_ALPHA * gt))
        part = jnp.dot(hidden.astype(bf16), wd_ref[...].astype(bf16), preferred_element_type=f32)

        @pl.when(f == 0)
        def _():
            y_ref[...] = part + bd_ref[...]

        @pl.when(f > 0)
        def _():
            y_ref[...] += part

    @pl.when(jnp.logical_and(jnp.logical_not(used), f == 0))
    def _():
        y_ref[...] = jnp.zeros_like(y_ref)


def _experts(block_expert, n_used, xs, wg, wu, wd, bg, bu, bd, bm, tf):
    n_rows = xs.shape[0]
    n_blocks = n_rows // bm
    n_f = D_FF // tf

    def blk(b, nu):
        return jnp.minimum(b, nu[0] - 1)

    def ftile(b, f, nu):
        return jnp.where(b < nu[0], f, n_f - 1)

    grid_spec = pltpu.PrefetchScalarGridSpec(
        num_scalar_prefetch=2,
        grid=(n_blocks, n_f),
        in_specs=[pl.BlockSpec((bm, D_MODEL), lambda b, f, be, nu: (blk(b, nu), 0)),
                  pl.BlockSpec((None, D_MODEL, tf), lambda b, f, be, nu: (be[blk(b, nu)], 0, ftile(b, f, nu))),
                  pl.BlockSpec((None, D_MODEL, tf), lambda b, f, be, nu: (be[blk(b, nu)], 0, ftile(b, f, nu))),
                  pl.BlockSpec((None, tf, D_MODEL), lambda b, f, be, nu: (be[blk(b, nu)], ftile(b, f, nu), 0)),
                  pl.BlockSpec((None, 1, tf), lambda b, f, be, nu: (be[blk(b, nu)], 0, ftile(b, f, nu))),
                  pl.BlockSpec((None, 1, tf), lambda b, f, be, nu: (be[blk(b, nu)], 0, ftile(b, f, nu))),
                  pl.BlockSpec((None, 1, D_MODEL), lambda b, f, be, nu: (be[blk(b, nu)], 0, 0))],
        out_specs=pl.BlockSpec((bm, D_MODEL), lambda b, f, be, nu: (b, 0)),
    )
    return pl.pallas_call(
        _expert_kernel,
        grid_spec=grid_spec,
        out_shape=jax.ShapeDtypeStruct((n_rows, D_MODEL), f32),
        compiler_params=_cparams(("arbitrary", "arbitrary")),
        name="moe_experts",
    )(block_expert, n_used, xs, wg, wu, wd, bg, bu, bd)


def _combine_kernel(dest_ref, y_hbm, rg_ref, h1_ref, g_ref, o_ref, buf_ref, sem, *, tm):
    def row_copy(j, k):
        return pltpu.make_async_copy(y_hbm.at[pl.ds(dest_ref[j * TOP_K + k], 1)],
                                     buf_ref.at[k, pl.ds(j, 1)], sem)

    def issue(j, carry):
        for k in range(TOP_K):
            row_copy(j, k).start()
        return carry

    def drain(j, carry):
        for k in range(TOP_K):
            row_copy(j, k).wait()
        return carry

    lax.fori_loop(0, tm, issue, 0)
    lax.fori_loop(0, tm, drain, 0)
    rg = rg_ref[...]
    h = h1_ref[...]
    for k in range(TOP_K):
        h = h + buf_ref[k] * rg[:, k:k + 1]
    o_ref[...] = h * lax.rsqrt(jnp.mean(h * h, axis=-1, keepdims=True) + EPS) * g_ref[...]


def _combine(dest_flat, y, rg, h1, g_fin, tm):
    n = h1.shape[0]
    return pl.pallas_call(
        functools.partial(_combine_kernel, tm=tm),
        grid=(n // tm,),
        in_specs=[pl.BlockSpec((tm * TOP_K,), lambda i: (i,), memory_space=pltpu.SMEM),
                  pl.BlockSpec(memory_space=pl.ANY),
                  pl.BlockSpec((tm, LANES), lambda i: (i, 0)),
                  pl.BlockSpec((tm, D_MODEL), lambda i: (i, 0)),
                  pl.BlockSpec((1, D_MODEL), lambda i: (0, 0))],
        out_specs=pl.BlockSpec((tm, D_MODEL), lambda i: (i, 0)),
        out_shape=jax.ShapeDtypeStruct((n, D_MODEL), f32),
        scratch_shapes=[pltpu.VMEM((TOP_K, tm, D_MODEL), f32), pltpu.SemaphoreType.DMA],
        compiler_params=_cparams(("arbitrary",)),
        name="moe_combine_norm",
    )(dest_flat, y, rg, h1, g_fin)


def _rope_tables(n_pos):
    freqs = ROPE_THETA ** (-jnp.arange(0, ROT_DIM, 2, dtype=f32) / ROT_DIM)
    ang = jnp.arange(n_pos, dtype=f32)[:, None] * freqs[None, :]
    cos, sin = jnp.cos(ang), jnp.sin(ang)
    ones = jnp.ones((n_pos, HEAD_DIM - ROT_DIM), f32)
    zeros8 = jnp.zeros((n_pos, ROT_HALF), f32)
    zeros48 = jnp.zeros((n_pos, HEAD_DIM - ROT_DIM), f32)
    c = jnp.concatenate([cos, cos, ones], axis=1)
    s1 = jnp.concatenate([zeros8, sin, zeros48], axis=1)
    s2 = jnp.concatenate([-sin, zeros8, zeros48], axis=1)
    tile2 = lambda t: jnp.concatenate([t, t], axis=1)
    return (tile2(c), tile2(s1), tile2(s2)), (cos.T, sin.T)


def kernel(x, meta_tokens, norm_mix_g, w_in, pool_w, pool_scale, w_branch_pool, w_branch_attn, w_out,
           norm_moe_g, router_w, router_b, exp_w_gate, exp_b_gate, exp_w_up, exp_b_up, exp_w_down,
           exp_b_down, norm_final_g):
    batch, seq, _ = x.shape
    n_tok = batch * seq
    n_sel = min(TOPK_KEYS_MAX, (seq + N_META) // 4)
    x2d = x.reshape(n_tok, D_MODEL)
    kc, mc = ATTN_KC, ATTN_MC

    w = w_in[0]
    o_u, o_q, o_k, o_v, o_iq = 0, POOL_WIDTH, POOL_WIDTH + ATTN_WIDTH, POOL_WIDTH + 2 * ATTN_WIDTH, POOL_WIDTH + 3 * ATTN_WIDTH
    o_ik = o_iq + N_IDX_HEADS * IDX_DIM
    o_iw = o_ik + IDX_DIM
    o_gp = o_iw + N_IDX_HEADS
    o_ga = o_gp + D_MODEL
    cast = lambda a: a.astype(bf16)
    w_u, w_q, w_v, w_iq = (cast(w[:, o:o + 1024]) for o in (o_u, o_q, o_v, o_iq))
    w_kT = cast(w[:, o_k:o_k + ATTN_WIDTH].T)
    w_ikT = cast(w[:, o_ik:o_ik + IDX_DIM].T)
    w_iw = cast(jnp.pad(w[:, o_iw:o_iw + N_IDX_HEADS], ((0, 0), (0, LANES - N_IDX_HEADS)))
                * (N_IDX_HEADS ** -0.5 * IDX_DIM ** -0.5))
    w_gates = cast(w[:, o_gp:o_ga + D_MODEL])

    (tab_tok, (cosT, sinT)) = _rope_tables(seq + N_META)
    tok_x = tuple(t[N_META:] for t in tab_tok)
    cosT_x, sinT_x = cosT[:, N_META:], sinT[:, N_META:]
    pad_cols = lambda t, n: jnp.pad(t[:, :n] if t.shape[1] >= n else t, ((0, 0), (0, max(0, n - t.shape[1]))))
    cosT_m, sinT_m = pad_cols(cosT, mc), pad_cols(sinT, mc)

    xn = _rmsnorm(x2d, norm_mix_g[0], NORM_TM)
    meta_pad = jnp.pad(meta_tokens.astype(f32), ((0, mc - N_META), (0, 0)))
    xn_meta = _rmsnorm(meta_pad, norm_mix_g[0], mc)

    u = _proj(xn, w_u, PROJ_TM, f32)
    v = _proj(xn, w_v, PROJ_TM, bf16)
    sg = _proj(xn, w_gates, PROJ_TM, bf16, act="sigmoid")
    iw = _proj(xn, w_iw, PROJ_TM, f32)
    q = _proj_rope(xn, w_q, tok_x, PROJ_TM, seq // PROJ_TM, HEAD_DIM ** -0.5)
    iq = _proj_rope(xn, w_iq, tok_x, PROJ_TM, seq // PROJ_TM, 1.0)
    kT = _projT_rope(xn, w_kT, cosT_x, sinT_x, kc, seq // kc, N_HEADS)
    ikT = _projT_rope(xn, w_ikT, cosT_x, sinT_x, kc, seq // kc, 1)

    u_meta = _proj(xn_meta, w_u, mc, f32)[:N_META]
    v_meta = _proj(xn_meta, w_v, mc, bf16)
    kT_meta = _projT_rope(xn_meta, w_kT, cosT_m, sinT_m, mc, 1, N_HEADS)[0]
    ikT_meta = _projT_rope(xn_meta, w_ikT, cosT_m, sinT_m, mc, 1, 1)[0]

    pp = _pool(u.reshape(batch, seq, POOL_WIDTH), u_meta, cast(pool_w[0]), pool_scale[0].reshape(1, POOL_WIDTH),
               POOL_TM).reshape(n_tok, POOL_WIDTH)
    at = _attention(q, iq, iw, kT.reshape(batch, seq // kc, ATTN_WIDTH, kc), v,
                    ikT.reshape(batch, seq // kc, IDX_DIM, kc), kT_meta, v_meta, ikT_meta, batch, seq, n_sel)

    rw = jnp.pad(router_w[0].astype(f32), ((0, 0), (0, LANES - N_EXPERTS)))
    rb = jnp.pad(router_b[0].astype(f32), (0, LANES - N_EXPERTS)).reshape(1, LANES)
    h1, hn, ri, rg, cnt = _post(x2d, pp, at, sg, sg, cast(w_branch_pool[0]),
                                cast(w_branch_attn[0]), cast(w_out[0]), norm_moe_g[0].reshape(1, D_MODEL),
                                rw, rb, POST_TM)

    bm = MOE_BM
    counts = cnt[0, :N_EXPERTS].astype(i32)
    padded = ((counts + bm - 1) // bm) * bm
    pad_end = jnp.cumsum(padded)
    pad_start = pad_end - padded
    n_blocks = -(-(n_tok * TOP_K + N_EXPERTS * (bm - 1)) // bm)
    dest = (pad_start[ri[:, :TOP_K]] + ri[:, TOP_K:2 * TOP_K]).reshape(n_tok * TOP_K)
    block_expert = jnp.minimum(jnp.searchsorted(pad_end, jnp.arange(n_blocks, dtype=i32) * bm, side="right"),
                               N_EXPERTS - 1).astype(i32)
    n_used = (pad_end[-1] // bm).astype(i32).reshape(1)

    row_tok = jnp.zeros((n_blocks * bm,), i32).at[dest].set(jnp.arange(n_tok * TOP_K, dtype=i32) // TOP_K)
    xs = _dispatch(n_used, row_tok, hn, bm)
    y = _experts(block_expert, n_used, xs, exp_w_gate[0], exp_w_up[0], exp_w_down[0],
                 exp_b_gate[0].reshape(N_EXPERTS, 1, D_FF), exp_b_up[0].reshape(N_EXPERTS, 1, D_FF),
                 exp_b_down[0].reshape(N_EXPERTS, 1, D_MODEL), bm, MOE_TF)
    out = _combine(dest, y, rg, h1, norm_final_g.reshape(1, D_MODEL), COMBINE_TM)
    return out.reshape(batch, seq, D_MODEL)
```

```python
import functools

import jax
import jax.numpy as jnp
from jax import lax
from jax.experimental import pallas as pl
from jax.experimental.pallas import tpu as pltpu

f32 = jnp.float32
bf16 = jnp.bfloat16
i32 = jnp.int32

D_MODEL = 2048
N_META = 16
POOL_WINDOWS = (2, 4, 8, 16)
N_POOL_GROUPS = 4
POOL_WIDTH = D_MODEL // 2
POOL_GROUP_DIM = POOL_WIDTH // N_POOL_GROUPS
N_HEADS = 16
HEAD_DIM = 64
ATTN_WIDTH = N_HEADS * HEAD_DIM
ROT_DIM = HEAD_DIM // 4
ROT_HALF = ROT_DIM // 2
ROPE_THETA = 500000.0
N_IDX_HEADS = 16
IDX_DIM = 64
TOPK_KEYS_MAX = 256
N_EXPERTS = 32
TOP_K = 4
D_FF = D_MODEL
SWIGLU_LIMIT = 7.0
SWIGLU_ALPHA = 1.702
EPS = 1e-5

LANES = 128
VMEM_LIMIT = 56 * 1024 * 1024

NORM_TM = 512
PROJ_TM = 512
ATTN_QB = 256
ATTN_KC = 512
ATTN_MC = 128
ATTN_GROUP = 4
POOL_TM = 512
POST_TM = 256
MOE_BM = 1024
MOE_TF = 256
COMBINE_TM = 256

NEG_BIG = -1e30
INT_MIN = -(2 ** 31)


def _cparams(sem):
    return pltpu.CompilerParams(dimension_semantics=sem, vmem_limit_bytes=VMEM_LIMIT)


def _rmsnorm_kernel(x_ref, g_ref, o_ref):
    x = x_ref[...]
    ms = jnp.mean(x * x, axis=-1, keepdims=True)
    o_ref[...] = (x * lax.rsqrt(ms + EPS) * g_ref[...]).astype(o_ref.dtype)


def _rmsnorm(x2d, g, tm):
    n = x2d.shape[0]
    return pl.pallas_call(
        _rmsnorm_kernel,
        grid=(n // tm,),
        in_specs=[pl.BlockSpec((tm, D_MODEL), lambda i: (i, 0)),
                  pl.BlockSpec((1, D_MODEL), lambda i: (0, 0))],
        out_specs=pl.BlockSpec((tm, D_MODEL), lambda i: (i, 0)),
        out_shape=jax.ShapeDtypeStruct((n, D_MODEL), bf16),
        compiler_params=_cparams(("parallel",)),
        name="rmsnorm",
    )(x2d, g.reshape(1, D_MODEL))


def _proj_kernel(xn_ref, w_ref, o_ref, *, act):
    y = jnp.dot(xn_ref[...], w_ref[...], preferred_element_type=f32)
    if act == "sigmoid":
        y = jax.nn.sigmoid(y)
    o_ref[...] = y.astype(o_ref.dtype)


def _proj(xn, w, tm, out_dtype, act=None, tn=1024):
    n, width = xn.shape[0], w.shape[1]
    tn = min(tn, width)
    return pl.pallas_call(
        functools.partial(_proj_kernel, act=act),
        grid=(width // tn, n // tm),
        in_specs=[pl.BlockSpec((tm, D_MODEL), lambda j, i: (i, 0)),
                  pl.BlockSpec((D_MODEL, tn), lambda j, i: (0, j))],
        out_specs=pl.BlockSpec((tm, tn), lambda j, i: (i, j)),
        out_shape=jax.ShapeDtypeStruct((n, width), out_dtype),
        compiler_params=_cparams(("parallel", "parallel")),
        name="proj_" + (act or "plain"),
    )(xn, w)


def _proj_rope_kernel(xn_ref, w_ref, c_ref, s1_ref, s2_ref, o_ref, *, scale):
    y = jnp.dot(xn_ref[...], w_ref[...], preferred_element_type=f32)
    c, s1, s2 = c_ref[...], s1_ref[...], s2_ref[...]
    for j in range(y.shape[1] // LANES):
        yt = y[:, j * LANES:(j + 1) * LANES]
        r = yt * c + pltpu.roll(yt, ROT_HALF, 1) * s1 + pltpu.roll(yt, LANES - ROT_HALF, 1) * s2
        o_ref[:, j * LANES:(j + 1) * LANES] = (r * scale).astype(o_ref.dtype)


def _proj_rope(xn, w, tables, tm, pos_blocks, scale):
    n, width = xn.shape[0], w.shape[1]
    tspec = pl.BlockSpec((tm, LANES), lambda i: (i % pos_blocks, 0))
    return pl.pallas_call(
        functools.partial(_proj_rope_kernel, scale=scale),
        grid=(n // tm,),
        in_specs=[pl.BlockSpec((tm, D_MODEL), lambda i: (i, 0)),
                  pl.BlockSpec((D_MODEL, width), lambda i: (0, 0)),
                  tspec, tspec, tspec],
        out_specs=pl.BlockSpec((tm, width), lambda i: (i, 0)),
        out_shape=jax.ShapeDtypeStruct((n, width), bf16),
        compiler_params=_cparams(("parallel",)),
        name="proj_rope",
    )(xn, w, *tables)


def _projT_rope_kernel(wT_ref, xn_ref, cos_ref, sin_ref, o_ref, *, n_heads):
    y = lax.dot_general(wT_ref[...], xn_ref[...], (((1,), (1,)), ((), ())),
                        preferred_element_type=f32)
    c, s = cos_ref[...], sin_ref[...]
    for h in range(n_heads):
        blk = y[h * HEAD_DIM:(h + 1) * HEAD_DIM]
        x1, x2 = blk[0:ROT_HALF], blk[ROT_HALF:ROT_DIM]
        out = jnp.concatenate([x1 * c - x2 * s, x2 * c + x1 * s, blk[ROT_DIM:]], axis=0)
        o_ref[h * HEAD_DIM:(h + 1) * HEAD_DIM, :] = out.astype(o_ref.dtype)


def _projT_rope(xn, wT, cosT, sinT, tm, pos_blocks, n_heads):
    n, rows = xn.shape[0], wT.shape[0]
    tspec = pl.BlockSpec((ROT_HALF, tm), lambda i: (0, i % pos_blocks))
    return pl.pallas_call(
        functools.partial(_projT_rope_kernel, n_heads=n_heads),
        grid=(n // tm,),
        in_specs=[pl.BlockSpec((rows, D_MODEL), lambda i: (0, 0)),
                  pl.BlockSpec((tm, D_MODEL), lambda i: (i, 0)),
                  tspec, tspec],
        out_specs=pl.BlockSpec((None, rows, tm), lambda i: (i, 0, 0)),
        out_shape=jax.ShapeDtypeStruct((n // tm, rows, tm), bf16),
        compiler_params=_cparams(("parallel",)),
        name="projT_rope",
    )(wT, xn, cosT, sinT)


def _pool_kernel(u_ref, prev_ref, meta_ref, pw_ref, ps_ref, o_ref, ext_ref, *, tm):
    i = pl.program_id(1)
    halo = jnp.where(i == 0, meta_ref[...], prev_ref[...])
    ext_ref[0:N_META, :] = halo
    ext_ref[N_META:, :] = u_ref[...]
    pos = i * tm + lax.broadcasted_iota(i32, (tm, 1), 0) + N_META
    for g, w in enumerate(POOL_WINDOWS):
        cols = slice(g * POOL_GROUP_DIM, (g + 1) * POOL_GROUP_DIM)
        acc = ext_ref[N_META:, cols]
        for j in range(1, w):
            acc = acc + ext_ref[pl.ds(N_META - j, tm), cols]
        count = jnp.minimum(pos + 1, w).astype(f32)
        pooled = acc / count - ext_ref[N_META:, cols]
        mixed = jnp.dot(pooled.astype(bf16), pw_ref[g], preferred_element_type=f32)
        o_ref[:, cols] = (mixed * ps_ref[:, cols]).astype(o_ref.dtype)


def _pool(u3, u_meta, pool_w, pool_scale, tm):
    b, s, _ = u3.shape
    per = tm // N_META
    return pl.pallas_call(
        functools.partial(_pool_kernel, tm=tm),
        grid=(b, s // tm),
        in_specs=[pl.BlockSpec((None, tm, POOL_WIDTH), lambda bb, i: (bb, i, 0)),
                  pl.BlockSpec((None, N_META, POOL_WIDTH),
                               lambda bb, i: (bb, jnp.maximum(i * per - 1, 0), 0)),
                  pl.BlockSpec((N_META, POOL_WIDTH), lambda bb, i: (0, 0)),
                  pl.BlockSpec((N_POOL_GROUPS, POOL_GROUP_DIM, POOL_GROUP_DIM), lambda bb, i: (0, 0, 0)),
                  pl.BlockSpec((1, POOL_WIDTH), lambda bb, i: (0, 0))],
        out_specs=pl.BlockSpec((None, tm, POOL_WIDTH), lambda bb, i: (bb, i, 0)),
        out_shape=jax.ShapeDtypeStruct((b, s, POOL_WIDTH), bf16),
        scratch_shapes=[pltpu.VMEM((tm + N_META, POOL_WIDTH), f32)],
        compiler_params=_cparams(("parallel", "parallel")),
        name="pool_mixer",
    )(u3, u3, u_meta, pool_w, pool_scale)


def _order_key(score):
    bits = pltpu.bitcast(score + 0.0, i32)
    return jnp.where(bits < 0, bits ^ jnp.int32(0x7FFFFFFF), bits)


def _attn_kernel(q_ref, iq_ref, iw_ref, kT_ref, v_ref, ikT_ref, kTm_ref, vm_ref, ikTm_ref,
                 o_ref, key_ref, keym_ref, iwb_ref, s_ref, sm_ref, mrun_ref, lrun_ref, acc_ref,
                 *, n_sel, n_chunks, ord_bits):
    qb, kc, mc = q_ref.shape[0], kT_ref.shape[2], kTm_ref.shape[1]
    half = qb // 2
    sub = 2 * LANES
    meta = n_chunks
    qi = pl.program_id(1)
    n_ch = (qi * qb + qb - 1) // kc + 1
    row = qi * qb + lax.broadcasted_iota(i32, (qb, 1), 0)

    def tile_lanes(a, width):
        return jnp.concatenate([a] * (width // LANES), axis=1) if width > LANES else a

    def lane_max(a):
        out = a[:, :LANES]
        for j in range(1, a.shape[1] // LANES):
            out = jnp.maximum(out, a[:, j * LANES:(j + 1) * LANES])
        return out

    def lane_sum(a):
        out = a[:, :LANES]
        for j in range(1, a.shape[1] // LANES):
            out = out + a[:, j * LANES:(j + 1) * LANES]
        return out

    iq_heads = [iq_ref[:, h * IDX_DIM:(h + 1) * IDX_DIM] for h in range(N_IDX_HEADS)]
    iw = iw_ref[...]
    for h in range(N_IDX_HEADS):
        iwb_ref[h] = jnp.broadcast_to(iw[:, h:h + 1], (qb, LANES))

    def index_keys(ikT_c, valid):
        width = ikT_c.shape[1]
        acc = jnp.zeros((qb, width), f32)
        for h in range(N_IDX_HEADS):
            lg = jnp.dot(iq_heads[h], ikT_c, preferred_element_type=f32)
            acc = acc + jnp.maximum(lg, 0.0) * tile_lanes(iwb_ref[h], width)
        return jnp.where(valid, _order_key(acc), INT_MIN)

    def score_body(c, carry):
        for j in range(kc // sub):
            col = c * kc + j * sub + lax.broadcasted_iota(i32, (1, sub), 1)
            key_ref[c, :, j * sub:(j + 1) * sub] = index_keys(ikT_ref[c, :, j * sub:(j + 1) * sub], col <= row)
        return carry

    lax.fori_loop(0, n_ch, score_body, 0)
    keym_ref[...] = index_keys(ikTm_ref[...], lax.broadcasted_iota(i32, (1, mc), 1) < N_META)

    def keys_of(c, rows=slice(None)):
        return keym_ref[rows, :] if isinstance(c, int) and c == meta else key_ref[c, rows, :]

    def store_bias(c, bias):
        if isinstance(c, int) and c == meta:
            keym_ref[...] = pltpu.bitcast(bias, i32)
        else:
            key_ref[c] = pltpu.bitcast(bias, i32)

    def count_rows(pred_fn, operands):
        halves = [slice(r * half, (r + 1) * half) for r in range(2)]
        bcast = [[jnp.broadcast_to(o[rows], (half, LANES)) for o in operands] for rows in halves]
        accs = []
        for rows, ops in zip(halves, bcast):
            def add(c, acc, rows=rows, ops=ops):
                k = keys_of(c, rows)
                for j in range(k.shape[1] // LANES):
                    hit = pred_fn(c, j, k[:, j * LANES:(j + 1) * LANES], *ops)
                    acc = acc + jnp.where(hit, 1.0, 0.0)
                return acc
            accs.append(lax.fori_loop(0, n_ch, add, add(meta, jnp.zeros((half, LANES), f32))))
        return jnp.sum(jnp.concatenate(accs, axis=0), axis=1, keepdims=True)

    def count_ge(cand):
        return count_rows(lambda c, j, k, cb: k >= cb, [cand])

    t0 = jnp.where(count_ge(jnp.zeros((qb, 1), i32)) >= n_sel, 0, INT_MIN).astype(i32)

    def bisect(it, t):
        cand = t + lax.shift_left(jnp.int32(1), 30 - it)
        return jnp.where(count_ge(cand) >= n_sel, cand, t)

    thr = lax.fori_loop(0, 31, bisect, t0)
    n_ge = count_ge(thr)
    tied = jnp.logical_and(n_ge > n_sel, thr > INT_MIN)
    any_tied = jnp.max(jnp.where(tied, 1.0, 0.0)) > 0.0
    floor_key = jnp.maximum(thr, INT_MIN + 1)

    def ordinal(c, lanes):
        return jnp.where(c == meta, lanes, c * kc + lanes + N_META)

    @pl.when(jnp.logical_not(any_tied))
    def _():
        def to_bias(c, carry):
            store_bias(c, jnp.where(keys_of(c) >= floor_key, 0.0, NEG_BIG).astype(f32))
            return carry
        lax.fori_loop(0, n_ch, to_bias, 0)
        to_bias(meta, 0)

    @pl.when(any_tied)
    def _():
        need = n_sel - count_ge(thr + 1)
        lane_t = lax.broadcasted_iota(i32, (1, LANES), 1)

        def tie_pred(c, j, k, tb, cb):
            return jnp.logical_and(k == tb, ordinal(c, lane_t + j * LANES) <= cb)

        def bisect_pos(it, lo):
            cand = lo + lax.shift_left(jnp.int32(1), ord_bits - 1 - it)
            cnt = count_rows(tie_pred, [thr, cand])
            return jnp.where(cnt < need, cand, lo)

        lo = lax.fori_loop(0, ord_bits, bisect_pos, jnp.full((qb, 1), -1, i32))
        last = jnp.where(tied, lo + 1, jnp.int32(2 ** 30))

        def to_bias(c, carry):
            k = keys_of(c)
            lanes = lax.broadcasted_iota(i32, (1, k.shape[1]), 1)
            keep = jnp.logical_and(k >= floor_key, jnp.logical_or(k != thr, ordinal(c, lanes) <= last))
            store_bias(c, jnp.where(keep, 0.0, NEG_BIG).astype(f32))
            return carry
        lax.fori_loop(0, n_ch, to_bias, 0)
        to_bias(meta, 0)

    n_grp = s_ref.shape[0]
    low_half = lax.broadcasted_iota(i32, (1, 2 * HEAD_DIM), 1) < HEAD_DIM
    for grp in range(N_HEADS // n_grp):
        heads = [grp * n_grp + g for g in range(n_grp)]
        q_heads = [q_ref[:, h * HEAD_DIM:(h + 1) * HEAD_DIM] for h in heads]
        mrun_ref[...] = jnp.full(mrun_ref.shape, NEG_BIG, f32)

        def sweep_scores(c, carry):
            bias = pltpu.bitcast(key_ref[c], f32)
            for g, h in enumerate(heads):
                s = jnp.dot(q_heads[g], kT_ref[c, h * HEAD_DIM:(h + 1) * HEAD_DIM, :],
                            preferred_element_type=f32) + bias
                s_ref[g, c] = s
                mrun_ref[g] = jnp.maximum(mrun_ref[g], lane_max(s))
            return carry

        lax.fori_loop(0, n_ch, sweep_scores, 0)
        bias_m = pltpu.bitcast(keym_ref[...], f32)
        for g, h in enumerate(heads):
            s = jnp.dot(q_heads[g], kTm_ref[h * HEAD_DIM:(h + 1) * HEAD_DIM, :], preferred_element_type=f32) + bias_m
            sm_ref[g] = s
            row_max = jnp.max(jnp.maximum(mrun_ref[g], lane_max(s)), axis=1, keepdims=True)
            mrun_ref[g] = jnp.broadcast_to(row_max, (qb, LANES))

        lrun_ref[...] = jnp.zeros(lrun_ref.shape, f32)
        acc_ref[...] = jnp.zeros(acc_ref.shape, f32)

        def accumulate(get_s, v_rows, width):
            for hp in range(n_grp // 2):
                lanes = slice((heads[0] + 2 * hp) * HEAD_DIM, (heads[0] + 2 * hp + 2) * HEAD_DIM)
                v_pair = v_rows[:, lanes]
                parts = []
                for g in (2 * hp, 2 * hp + 1):
                    p = jnp.exp(get_s(g) - tile_lanes(mrun_ref[g], width))
                    lrun_ref[g] = lrun_ref[g] + lane_sum(p)
                    parts.append(jnp.dot(p.astype(bf16), v_pair, preferred_element_type=f32))
                out = slice(2 * hp * HEAD_DIM, (2 * hp + 2) * HEAD_DIM)
                acc_ref[:, out] = acc_ref[:, out] + jnp.where(low_half, parts[0], parts[1])

        def sweep_values(c, carry):
            start = pl.multiple_of(c * kc, kc)
            accumulate(lambda g: s_ref[g, c], v_ref.at[pl.ds(start, kc)], kc)
            return carry

        lax.fori_loop(0, n_ch, sweep_values, 0)
        accumulate(lambda g: sm_ref[g], vm_ref, mc)
        for hp in range(n_grp // 2):
            out = slice(2 * hp * HEAD_DIM, (2 * hp + 2) * HEAD_DIM)
            denom = jnp.where(low_half, jnp.sum(lrun_ref[2 * hp], axis=1, keepdims=True),
                              jnp.sum(lrun_ref[2 * hp + 1], axis=1, keepdims=True))
            lanes = slice((heads[0] + 2 * hp) * HEAD_DIM, (heads[0] + 2 * hp + 2) * HEAD_DIM)
            o_ref[:, lanes] = (acc_ref[:, out] / denom).astype(o_ref.dtype)


def _attention(q, iq, iw, kT, v, ikT, kT_meta, v_meta, ikT_meta, batch, seq, n_sel):
    qb, kc, mc, grp = ATTN_QB, ATTN_KC, ATTN_MC, ATTN_GROUP
    n_q = seq // qb
    n_chunks = seq // kc
    ord_bits = max(1, (seq + N_META - 1).bit_length())
    kernel = functools.partial(_attn_kernel, n_sel=n_sel, n_chunks=n_chunks, ord_bits=ord_bits)
    once = pl.Buffered(1)
    return pl.pallas_call(
        kernel,
        grid=(batch, n_q),
        in_specs=[pl.BlockSpec((qb, ATTN_WIDTH), lambda b, i: (b * n_q + i, 0)),
                  pl.BlockSpec((qb, N_IDX_HEADS * IDX_DIM), lambda b, i: (b * n_q + i, 0)),
                  pl.BlockSpec((qb, LANES), lambda b, i: (b * n_q + i, 0)),
                  pl.BlockSpec((None, n_chunks, ATTN_WIDTH, kc), lambda b, i: (b, 0, 0, 0), pipeline_mode=once),
                  pl.BlockSpec((seq, ATTN_WIDTH), lambda b, i: (b, 0), pipeline_mode=once),
                  pl.BlockSpec((None, n_chunks, IDX_DIM, kc), lambda b, i: (b, 0, 0, 0), pipeline_mode=once),
                  pl.BlockSpec((ATTN_WIDTH, mc), lambda b, i: (0, 0)),
                  pl.BlockSpec((mc, ATTN_WIDTH), lambda b, i: (0, 0)),
                  pl.BlockSpec((IDX_DIM, mc), lambda b, i: (0, 0))],
        out_specs=pl.BlockSpec((qb, ATTN_WIDTH), lambda b, i: (b * n_q + i, 0)),
        out_shape=jax.ShapeDtypeStruct((batch * seq, ATTN_WIDTH), bf16),
        scratch_shapes=[pltpu.VMEM((n_chunks, qb, kc), i32),
                        pltpu.VMEM((qb, mc), i32),
                        pltpu.VMEM((N_IDX_HEADS, qb, LANES), f32),
                        pltpu.VMEM((grp, n_chunks, qb, kc), f32),
                        pltpu.VMEM((grp, qb, mc), f32),
                        pltpu.VMEM((grp, qb, LANES), f32),
                        pltpu.VMEM((grp, qb, LANES), f32),
                        pltpu.VMEM((qb, grp * HEAD_DIM), f32)],
        compiler_params=_cparams(("parallel", "arbitrary")),
        name="indexed_attention",
    )(q, iq, iw, kT, v, ikT, kT_meta, v_meta, ikT_meta)


def _post_kernel(x_ref, pp_ref, at_ref, sgp_ref, sga_ref, wbp_ref, wba_ref, wout_ref, g_ref, rw_ref, rb_ref,
                 h1_ref, hn_ref, ri_ref, rg_ref, cnt_ref, carry_ref):
    tm = x_ref.shape[0]

    @pl.when(pl.program_id(0) == 0)
    def _():
        carry_ref[...] = jnp.zeros_like(carry_ref)

    y_pool = jnp.dot(pp_ref[...], wbp_ref[...], preferred_element_type=f32)
    y_attn = jnp.dot(at_ref[...], wba_ref[...], preferred_element_type=f32)
    merged = sgp_ref[...].astype(f32) * y_pool + sga_ref[...].astype(f32) * y_attn
    h1 = x_ref[...] + jnp.dot(merged.astype(bf16), wout_ref[...], preferred_element_type=f32)
    h1_ref[...] = h1
    hn = h1 * lax.rsqrt(jnp.mean(h1 * h1, axis=-1, keepdims=True) + EPS) * g_ref[...]
    hn_ref[...] = _pack_bf16_pairs(hn[:, :D_MODEL // 2], hn[:, D_MODEL // 2:])

    lane = lax.broadcasted_iota(i32, (tm, LANES), 1)
    logits = jnp.dot(hn, rw_ref[...], preferred_element_type=f32, precision=lax.Precision.HIGHEST) + rb_ref[...]
    logits = jnp.where(lane < N_EXPERTS, logits, -jnp.inf)
    vals, idxs = [], []
    for _ in range(TOP_K):
        m = jnp.max(logits, axis=1, keepdims=True)
        idx = jnp.min(jnp.where(logits == m, lane, LANES), axis=1, keepdims=True)
        vals.append(m)
        idxs.append(idx)
        logits = jnp.where(lane == idx, -jnp.inf, logits)
    exps = [jnp.exp(v - vals[0]) for v in vals]
    denom = exps[0] + exps[1] + exps[2] + exps[3]

    hot = jnp.zeros((tm, LANES), f32)
    for idx in idxs:
        hot = hot + jnp.where(lane == idx, 1.0, 0.0)
    r_i = lax.broadcasted_iota(i32, (tm, tm), 0)
    c_i = lax.broadcasted_iota(i32, (tm, tm), 1)
    below = jnp.where(c_i < r_i, 1.0, 0.0).astype(bf16)
    before = jnp.dot(below, hot.astype(bf16), preferred_element_type=f32) + carry_ref[...]
    ri = jnp.zeros((tm, LANES), i32)
    rg = jnp.zeros((tm, LANES), f32)
    for k in range(TOP_K):
        rank = jnp.sum(jnp.where(lane == idxs[k], before, 0.0), axis=1, keepdims=True).astype(i32)
        ri = jnp.where(lane == k, idxs[k], ri)
        ri = jnp.where(lane == TOP_K + k, rank, ri)
        rg = jnp.where(lane == k, exps[k] / denom, rg)
    ri_ref[...] = ri
    rg_ref[...] = rg
    carry_ref[...] = carry_ref[...] + jnp.sum(hot, axis=0, keepdims=True)
    cnt_ref[...] = carry_ref[...]


def _post(x2d, pp, at, sgp, sga, wbp, wba, wout, g_moe, rw, rb, tm):
    n = x2d.shape[0]
    once = pl.Buffered(1)
    row = lambda w: pl.BlockSpec((tm, w), lambda i: (i, 0))
    full = lambda a, c: pl.BlockSpec((a, c), lambda i: (0, 0), pipeline_mode=once)
    return pl.pallas_call(
        _post_kernel,
        grid=(n // tm,),
        in_specs=[row(D_MODEL), row(POOL_WIDTH), row(ATTN_WIDTH), row(D_MODEL),
                  pl.BlockSpec((tm, D_MODEL), lambda i: (i, 1)),
                  full(POOL_WIDTH, D_MODEL), full(ATTN_WIDTH, D_MODEL), full(D_MODEL, D_MODEL),
                  full(1, D_MODEL), full(D_MODEL, LANES), full(1, LANES)],
        out_specs=[row(D_MODEL), row(D_MODEL // 2), row(LANES), row(LANES),
                   pl.BlockSpec((1, LANES), lambda i: (0, 0))],
        out_shape=[jax.ShapeDtypeStruct((n, D_MODEL), f32), jax.ShapeDtypeStruct((n, D_MODEL // 2), jnp.uint32),
                   jax.ShapeDtypeStruct((n, LANES), i32), jax.ShapeDtypeStruct((n, LANES), f32),
                   jax.ShapeDtypeStruct((1, LANES), f32)],
        scratch_shapes=[pltpu.VMEM((1, LANES), f32)],
        compiler_params=_cparams(("arbitrary",)),
        name="merge_outproj_router",
    )(x2d, pp, at, sgp, sga, wbp, wba, wout, g_moe, rw, rb)


def _pack_bf16_pairs(hi, lo):
    hi_bits = pltpu.bitcast(hi.astype(jnp.bfloat16).astype(f32), jnp.uint32)
    lo_bits = pltpu.bitcast(lo.astype(jnp.bfloat16).astype(f32), jnp.uint32)
    return hi_bits | (lo_bits >> 16)


def _unpack_bf16_pairs(words):
    hi = pltpu.bitcast(words & jnp.uint32(0xFFFF0000), f32)
    lo = pltpu.bitcast(words << 16, f32)
    return hi, lo


def _expert_kernel(be_ref, nu_ref, tok_ref, tok_next_ref, hn_hbm, wg_ref, wu_ref, wd_ref, bg_ref, bu_ref, bd_ref,
                   y_ref, xbuf_ref, xb_ref, sems, *, n_blocks):
    b, f = pl.program_id(0), pl.program_id(1)
    bm = y_ref.shape[0]
    n_f = D_FF // wg_ref.shape[1]
    per_step = bm // n_f
    used = b < nu_ref[0]
    slot = b % 2

    def row_copy(tok, j, s):
        return pltpu.make_async_copy(hn_hbm.at[pl.ds(tok, 1)], xbuf_ref.at[s, pl.ds(j, 1)], sems.at[s])

    def wait_rows(s):
        pltpu.make_async_copy(xbuf_ref.at[s], xbuf_ref.at[s], sems.at[s]).wait()

    @pl.when(jnp.logical_and(b == 0, f == 0))
    def _():
        def issue(j, carry):
            row_copy(tok_ref[j], j, 0).start()
            return carry
        lax.fori_loop(0, bm, issue, 0)

    @pl.when(jnp.logical_and(f == 0, b <= nu_ref[0]))
    def _():
        wait_rows(slot)

    @pl.when(jnp.logical_and(used, f == 0))
    def _():
        hi, lo = _unpack_bf16_pairs(xbuf_ref[slot])
        xb_ref[:, :D_MODEL // 2] = hi.astype(bf16)
        xb_ref[:, D_MODEL // 2:] = lo.astype(bf16)
        y_ref[...] = jnp.broadcast_to(bd_ref[...], y_ref.shape)

    @pl.when(used)
    def _():
        for j in range(per_step):
            r = f * per_step + j
            row_copy(tok_next_ref[r], r, 1 - slot).start()
        x = xb_ref[...]
        gt = jnp.dot(x, wg_ref[...].astype(bf16), preferred_element_type=f32) + bg_ref[...]
        up = jnp.dot(x, wu_ref[...].astype(bf16), preferred_element_type=f32) + bu_ref[...]
        gt = jnp.minimum(gt, SWIGLU_LIMIT)
        up = jnp.clip(up, -SWIGLU_LIMIT, SWIGLU_LIMIT)
        hidden = (up + 1.0) * (gt * jax.nn.sigmoid(SWIGLU_ALPHA * gt))
        y_ref[...] += jnp.dot(hidden.astype(bf16), wd_ref[...].astype(bf16), preferred_element_type=f32)

    @pl.when(jnp.logical_and(used, jnp.logical_and(b == n_blocks - 1, f == n_f - 1)))
    def _():
        wait_rows(1 - slot)

    @pl.when(jnp.logical_and(jnp.logical_not(used), f == 0))
    def _():
        y_ref[...] = jnp.zeros_like(y_ref)


def _experts(block_expert, n_used, row_tok, hn_words, wg, wu, wd, bg, bu, bd, bm, tf):
    n_rows = row_tok.shape[0]
    n_blocks = n_rows // bm
    n_f = D_FF // tf

    def blk(b, nu):
        return jnp.minimum(b, nu[0] - 1)

    def ftile(b, f, nu):
        return jnp.where(b < nu[0], f, n_f - 1)

    grid_spec = pltpu.PrefetchScalarGridSpec(
        num_scalar_prefetch=2,
        grid=(n_blocks, n_f),
        in_specs=[pl.BlockSpec((bm,), lambda b, f, be, nu: (b,), memory_space=pltpu.SMEM),
                  pl.BlockSpec((bm,), lambda b, f, be, nu: (jnp.minimum(b + 1, n_blocks - 1),),
                               memory_space=pltpu.SMEM),
                  pl.BlockSpec(memory_space=pl.ANY),
                  pl.BlockSpec((None, D_MODEL, tf), lambda b, f, be, nu: (be[blk(b, nu)], 0, ftile(b, f, nu))),
                  pl.BlockSpec((None, D_MODEL, tf), lambda b, f, be, nu: (be[blk(b, nu)], 0, ftile(b, f, nu))),
                  pl.BlockSpec((None, tf, D_MODEL), lambda b, f, be, nu: (be[blk(b, nu)], ftile(b, f, nu), 0)),
                  pl.BlockSpec((None, 1, tf), lambda b, f, be, nu: (be[blk(b, nu)], 0, ftile(b, f, nu))),
                  pl.BlockSpec((None, 1, tf), lambda b, f, be, nu: (be[blk(b, nu)], 0, ftile(b, f, nu))),
                  pl.BlockSpec((None, 1, D_MODEL), lambda b, f, be, nu: (be[blk(b, nu)], 0, 0))],
        out_specs=pl.BlockSpec((bm, D_MODEL), lambda b, f, be, nu: (b, 0)),
        scratch_shapes=[pltpu.VMEM((2, bm, D_MODEL // 2), jnp.uint32),
                        pltpu.VMEM((bm, D_MODEL), bf16),
                        pltpu.SemaphoreType.DMA((2,))],
    )
    return pl.pallas_call(
        functools.partial(_expert_kernel, n_blocks=n_blocks),
        grid_spec=grid_spec,
        out_shape=jax.ShapeDtypeStruct((n_rows, D_MODEL), f32),
        compiler_params=_cparams(("arbitrary", "arbitrary")),
        name="moe_experts",
    )(block_expert, n_used, row_tok, row_tok, hn_words, wg, wu, wd, bg, bu, bd)


def _combine_kernel(dest_ref, y_hbm, rg_ref, h1_ref, g_ref, o_ref, buf_ref, sem, *, tm):
    def row_copy(j, k):
        return pltpu.make_async_copy(y_hbm.at[pl.ds(dest_ref[j * TOP_K + k], 1)],
                                     buf_ref.at[k, pl.ds(j, 1)], sem)

    def issue(j, carry):
        for k in range(TOP_K):
            row_copy(j, k).start()
        return carry

    def drain(j, carry):
        for k in range(TOP_K):
            row_copy(j, k).wait()
        return carry

    lax.fori_loop(0, tm, issue, 0)
    lax.fori_loop(0, tm, drain, 0)
    rg = rg_ref[...]
    h = h1_ref[...]
    for k in range(TOP_K):
        h = h + buf_ref[k] * rg[:, k:k + 1]
    o_ref[...] = h * lax.rsqrt(jnp.mean(h * h, axis=-1, keepdims=True) + EPS) * g_ref[...]


def _combine(dest_flat, y, rg, h1, g_fin, tm):
    n = h1.shape[0]
    return pl.pallas_call(
        functools.partial(_combine_kernel, tm=tm),
        grid=(n // tm,),
        in_specs=[pl.BlockSpec((tm * TOP_K,), lambda i: (i,), memory_space=pltpu.SMEM),
                  pl.BlockSpec(memory_space=pl.ANY),
                  pl.BlockSpec((tm, LANES), lambda i: (i, 0)),
                  pl.BlockSpec((tm, D_MODEL), lambda i: (i, 0)),
                  pl.BlockSpec((1, D_MODEL), lambda i: (0, 0))],
        out_specs=pl.BlockSpec((tm, D_MODEL), lambda i: (i, 0)),
        out_shape=jax.ShapeDtypeStruct((n, D_MODEL), f32),
        scratch_shapes=[pltpu.VMEM((TOP_K, tm, D_MODEL), f32), pltpu.SemaphoreType.DMA],
        compiler_params=_cparams(("arbitrary",)),
        name="moe_combine_norm",
    )(dest_flat, y, rg, h1, g_fin)


def _rope_tables(n_pos):
    freqs = ROPE_THETA ** (-jnp.arange(0, ROT_DIM, 2, dtype=f32) / ROT_DIM)
    ang = jnp.arange(n_pos, dtype=f32)[:, None] * freqs[None, :]
    cos, sin = jnp.cos(ang), jnp.sin(ang)
    ones = jnp.ones((n_pos, HEAD_DIM - ROT_DIM), f32)
    zeros8 = jnp.zeros((n_pos, ROT_HALF), f32)
    zeros48 = jnp.zeros((n_pos, HEAD_DIM - ROT_DIM), f32)
    c = jnp.concatenate([cos, cos, ones], axis=1)
    s1 = jnp.concatenate([zeros8, sin, zeros48], axis=1)
    s2 = jnp.concatenate([-sin, zeros8, zeros48], axis=1)
    tile2 = lambda t: jnp.concatenate([t, t], axis=1)
    return (tile2(c), tile2(s1), tile2(s2)), (cos.T, sin.T)


def kernel(x, meta_tokens, norm_mix_g, w_in, pool_w, pool_scale, w_branch_pool, w_branch_attn, w_out,
           norm_moe_g, router_w, router_b, exp_w_gate, exp_b_gate, exp_w_up, exp_b_up, exp_w_down,
           exp_b_down, norm_final_g):
    batch, seq, _ = x.shape
    n_tok = batch * seq
    n_sel = min(TOPK_KEYS_MAX, (seq + N_META) // 4)
    x2d = x.reshape(n_tok, D_MODEL)
    kc, mc = ATTN_KC, ATTN_MC

    w = w_in[0]
    o_u, o_q, o_k, o_v, o_iq = 0, POOL_WIDTH, POOL_WIDTH + ATTN_WIDTH, POOL_WIDTH + 2 * ATTN_WIDTH, POOL_WIDTH + 3 * ATTN_WIDTH
    o_ik = o_iq + N_IDX_HEADS * IDX_DIM
    o_iw = o_ik + IDX_DIM
    o_gp = o_iw + N_IDX_HEADS
    o_ga = o_gp + D_MODEL
    cast = lambda a: a.astype(bf16)
    w_u, w_q, w_v, w_iq = (cast(w[:, o:o + 1024]) for o in (o_u, o_q, o_v, o_iq))
    w_kT = cast(w[:, o_k:o_k + ATTN_WIDTH].T)
    w_ikT = cast(w[:, o_ik:o_ik + IDX_DIM].T)
    w_iw = cast(jnp.pad(w[:, o_iw:o_iw + N_IDX_HEADS], ((0, 0), (0, LANES - N_IDX_HEADS)))
                * (N_IDX_HEADS ** -0.5 * IDX_DIM ** -0.5))
    w_gates = cast(w[:, o_gp:o_ga + D_MODEL])

    (tab_tok, (cosT, sinT)) = _rope_tables(seq + N_META)
    tok_x = tuple(t[N_META:] for t in tab_tok)
    cosT_x, sinT_x = cosT[:, N_META:], sinT[:, N_META:]
    pad_cols = lambda t, n: jnp.pad(t[:, :n] if t.shape[1] >= n else t, ((0, 0), (0, max(0, n - t.shape[1]))))
    cosT_m, sinT_m = pad_cols(cosT, mc), pad_cols(sinT, mc)

    xn = _rmsnorm(x2d, norm_mix_g[0], NORM_TM)
    meta_pad = jnp.pad(meta_tokens.astype(f32), ((0, mc - N_META), (0, 0)))
    xn_meta = _rmsnorm(meta_pad, norm_mix_g[0], mc)

    u = _proj(xn, w_u, PROJ_TM, f32)
    v = _proj(xn, w_v, PROJ_TM, bf16)
    sg = _proj(xn, w_gates, PROJ_TM, bf16, act="sigmoid")
    iw = _proj(xn, w_iw, PROJ_TM, f32)
    q = _proj_rope(xn, w_q, tok_x, PROJ_TM, seq // PROJ_TM, HEAD_DIM ** -0.5)
    iq = _proj_rope(xn, w_iq, tok_x, PROJ_TM, seq // PROJ_TM, 1.0)
    kT = _projT_rope(xn, w_kT, cosT_x, sinT_x, kc, seq // kc, N_HEADS)
    ikT = _projT_rope(xn, w_ikT, cosT_x, sinT_x, kc, seq // kc, 1)

    u_meta = _proj(xn_meta, w_u, mc, f32)[:N_META]
    v_meta = _proj(xn_meta, w_v, mc, bf16)
    kT_meta = _projT_rope(xn_meta, w_kT, cosT_m, sinT_m, mc, 1, N_HEADS)[0]
    ikT_meta = _projT_rope(xn_meta, w_ikT, cosT_m, sinT_m, mc, 1, 1)[0]

    pp = _pool(u.reshape(batch, seq, POOL_WIDTH), u_meta, cast(pool_w[0]), pool_scale[0].reshape(1, POOL_WIDTH),
               POOL_TM).reshape(n_tok, POOL_WIDTH)
    at = _attention(q, iq, iw, kT.reshape(batch, seq // kc, ATTN_WIDTH, kc), v,
                    ikT.reshape(batch, seq // kc, IDX_DIM, kc), kT_meta, v_meta, ikT_meta, batch, seq, n_sel)

    rw = jnp.pad(router_w[0].astype(f32), ((0, 0), (0, LANES - N_EXPERTS)))
    rb = jnp.pad(router_b[0].astype(f32), (0, LANES - N_EXPERTS)).reshape(1, LANES)
    h1, hn, ri, rg, cnt = _post(x2d, pp, at, sg, sg, cast(w_branch_pool[0]),
                                cast(w_branch_attn[0]), cast(w_out[0]), norm_moe_g[0].reshape(1, D_MODEL),
                                rw, rb, POST_TM)

    bm = MOE_BM
    counts = cnt[0, :N_EXPERTS].astype(i32)
    padded = ((counts + bm - 1) // bm) * bm
    pad_end = jnp.cumsum(padded)
    pad_start = pad_end - padded
    n_blocks = -(-(n_tok * TOP_K + N_EXPERTS * (bm - 1)) // bm)
    dest = (pad_start[ri[:, :TOP_K]] + ri[:, TOP_K:2 * TOP_K]).reshape(n_tok * TOP_K)
    block_expert = jnp.minimum(jnp.searchsorted(pad_end, jnp.arange(n_blocks, dtype=i32) * bm, side="right"),
                               N_EXPERTS - 1).astype(i32)
    n_used = (pad_end[-1] // bm).astype(i32).reshape(1)

    row_tok = jnp.zeros((n_blocks * bm,), i32).at[dest].set(jnp.arange(n_tok * TOP_K, dtype=i32) // TOP_K)
    y = _experts(block_expert, n_used, row_tok, hn, exp_w_gate[0], exp_w_up[0], exp_w_down[0],
                 exp_b_gate[0].reshape(N_EXPERTS, 1, D_FF), exp_b_up[0].reshape(N_EXPERTS, 1, D_FF),
                 exp_b_down[0].reshape(N_EXPERTS, 1, D_MODEL), bm, MOE_TF)
    out = _combine(dest, y, rg, h1, norm_final_g.reshape(1, D_MODEL), COMBINE_TM)
    return out.reshape(batch, seq, D_MODEL)
```

```python
import functools

import jax
import jax.numpy as jnp
from jax import lax
from jax.experimental import pallas as pl
from jax.experimental.pallas import tpu as pltpu

f32 = jnp.float32
bf16 = jnp.bfloat16
i32 = jnp.int32

D_MODEL = 2048
N_META = 16
POOL_WINDOWS = (2, 4, 8, 16)
N_POOL_GROUPS = 4
POOL_WIDTH = D_MODEL // 2
POOL_GROUP_DIM = POOL_WIDTH // N_POOL_GROUPS
N_HEADS = 16
HEAD_DIM = 64
ATTN_WIDTH = N_HEADS * HEAD_DIM
ROT_DIM = HEAD_DIM // 4
ROT_HALF = ROT_DIM // 2
ROPE_THETA = 500000.0
N_IDX_HEADS = 16
IDX_DIM = 64
TOPK_KEYS_MAX = 256
N_EXPERTS = 32
TOP_K = 4
D_FF = D_MODEL
SWIGLU_LIMIT = 7.0
SWIGLU_ALPHA = 1.702
EPS = 1e-5

LANES = 128
VMEM_LIMIT = 56 * 1024 * 1024

NORM_TM = 512
PROJ_TM = 512
ATTN_QB = 256
ATTN_KC = 512
ATTN_MC = 128
ATTN_GROUP = 4
POOL_TM = 512
POST_TM = 256
ROUTER_TM = 512
MOE_BM = 1024
MOE_TF = 256
COMBINE_TM = 256

LOG2_E = 1.4426950408889634
NEG_BIG = -1e30
INT_MIN = -(2 ** 31)


def _cparams(sem):
    return pltpu.CompilerParams(dimension_semantics=sem, vmem_limit_bytes=VMEM_LIMIT)


def _rmsnorm_kernel(x_ref, g_ref, o_ref):
    x = x_ref[...]
    ms = jnp.mean(x * x, axis=-1, keepdims=True)
    o_ref[...] = (x * lax.rsqrt(ms + EPS) * g_ref[...]).astype(o_ref.dtype)


def _rmsnorm(x2d, g, tm):
    n = x2d.shape[0]
    return pl.pallas_call(
        _rmsnorm_kernel,
        grid=(n // tm,),
        in_specs=[pl.BlockSpec((tm, D_MODEL), lambda i: (i, 0)),
                  pl.BlockSpec((1, D_MODEL), lambda i: (0, 0))],
        out_specs=pl.BlockSpec((tm, D_MODEL), lambda i: (i, 0)),
        out_shape=jax.ShapeDtypeStruct((n, D_MODEL), bf16),
        compiler_params=_cparams(("parallel",)),
        name="rmsnorm",
    )(x2d, g.reshape(1, D_MODEL))


def _proj_kernel(xn_ref, w_ref, o_ref, *, act):
    y = jnp.dot(xn_ref[...], w_ref[...], preferred_element_type=f32)
    if act == "sigmoid":
        y = jax.nn.sigmoid(y)
    o_ref[...] = y.astype(o_ref.dtype)


def _proj(xn, w, tm, out_dtype, act=None, tn=1024):
    n, width = xn.shape[0], w.shape[1]
    tn = min(tn, width)
    return pl.pallas_call(
        functools.partial(_proj_kernel, act=act),
        grid=(width // tn, n // tm),
        in_specs=[pl.BlockSpec((tm, D_MODEL), lambda j, i: (i, 0)),
                  pl.BlockSpec((D_MODEL, tn), lambda j, i: (0, j))],
        out_specs=pl.BlockSpec((tm, tn), lambda j, i: (i, j)),
        out_shape=jax.ShapeDtypeStruct((n, width), out_dtype),
        compiler_params=_cparams(("parallel", "parallel")),
        name="proj_" + (act or "plain"),
    )(xn, w)


def _proj_rope_kernel(xn_ref, w_ref, c_ref, s1_ref, s2_ref, o_ref, *, scale):
    y = jnp.dot(xn_ref[...], w_ref[...], preferred_element_type=f32)
    c, s1, s2 = c_ref[...], s1_ref[...], s2_ref[...]
    for j in range(y.shape[1] // LANES):
        yt = y[:, j * LANES:(j + 1) * LANES]
        r = yt * c + pltpu.roll(yt, ROT_HALF, 1) * s1 + pltpu.roll(yt, LANES - ROT_HALF, 1) * s2
        o_ref[:, j * LANES:(j + 1) * LANES] = (r * scale).astype(o_ref.dtype)


def _proj_rope(xn, w, tables, tm, pos_blocks, scale):
    n, width = xn.shape[0], w.shape[1]
    tspec = pl.BlockSpec((tm, LANES), lambda i: (i % pos_blocks, 0))
    return pl.pallas_call(
        functools.partial(_proj_rope_kernel, scale=scale),
        grid=(n // tm,),
        in_specs=[pl.BlockSpec((tm, D_MODEL), lambda i: (i, 0)),
                  pl.BlockSpec((D_MODEL, width), lambda i: (0, 0)),
                  tspec, tspec, tspec],
        out_specs=pl.BlockSpec((tm, width), lambda i: (i, 0)),
        out_shape=jax.ShapeDtypeStruct((n, width), bf16),
        compiler_params=_cparams(("parallel",)),
        name="proj_rope",
    )(xn, w, *tables)


def _projT_rope_kernel(wT_ref, xn_ref, cos_ref, sin_ref, o_ref, *, n_heads):
    y = lax.dot_general(wT_ref[...], xn_ref[...], (((1,), (1,)), ((), ())),
                        preferred_element_type=f32)
    c, s = cos_ref[...], sin_ref[...]
    for h in range(n_heads):
        blk = y[h * HEAD_DIM:(h + 1) * HEAD_DIM]
        x1, x2 = blk[0:ROT_HALF], blk[ROT_HALF:ROT_DIM]
        out = jnp.concatenate([x1 * c - x2 * s, x2 * c + x1 * s, blk[ROT_DIM:]], axis=0)
        o_ref[h * HEAD_DIM:(h + 1) * HEAD_DIM, :] = out.astype(o_ref.dtype)


def _projT_rope(xn, wT, cosT, sinT, tm, pos_blocks, n_heads):
    n, rows = xn.shape[0], wT.shape[0]
    tspec = pl.BlockSpec((ROT_HALF, tm), lambda i: (0, i % pos_blocks))
    return pl.pallas_call(
        functools.partial(_projT_rope_kernel, n_heads=n_heads),
        grid=(n // tm,),
        in_specs=[pl.BlockSpec((rows, D_MODEL), lambda i: (0, 0)),
                  pl.BlockSpec((tm, D_MODEL), lambda i: (i, 0)),
                  tspec, tspec],
        out_specs=pl.BlockSpec((None, rows, tm), lambda i: (i, 0, 0)),
        out_shape=jax.ShapeDtypeStruct((n // tm, rows, tm), bf16),
        compiler_params=_cparams(("parallel",)),
        name="projT_rope",
    )(wT, xn, cosT, sinT)


def _pool_kernel(u_ref, prev_ref, meta_ref, pw_ref, ps_ref, o_ref, ext_ref, *, tm):
    i = pl.program_id(1)
    halo = jnp.where(i == 0, meta_ref[...], prev_ref[...])
    ext_ref[0:N_META, :] = halo
    ext_ref[N_META:, :] = u_ref[...]
    pos = i * tm + lax.broadcasted_iota(i32, (tm, 1), 0) + N_META
    for g, w in enumerate(POOL_WINDOWS):
        cols = slice(g * POOL_GROUP_DIM, (g + 1) * POOL_GROUP_DIM)
        acc = ext_ref[N_META:, cols]
        for j in range(1, w):
            acc = acc + ext_ref[pl.ds(N_META - j, tm), cols]
        count = jnp.minimum(pos + 1, w).astype(f32)
        pooled = acc / count - ext_ref[N_META:, cols]
        mixed = jnp.dot(pooled.astype(bf16), pw_ref[g], preferred_element_type=f32)
        o_ref[:, cols] = (mixed * ps_ref[:, cols]).astype(o_ref.dtype)


def _pool(u3, u_meta, pool_w, pool_scale, tm):
    b, s, _ = u3.shape
    per = tm // N_META
    return pl.pallas_call(
        functools.partial(_pool_kernel, tm=tm),
        grid=(b, s // tm),
        in_specs=[pl.BlockSpec((None, tm, POOL_WIDTH), lambda bb, i: (bb, i, 0)),
                  pl.BlockSpec((None, N_META, POOL_WIDTH),
                               lambda bb, i: (bb, jnp.maximum(i * per - 1, 0), 0)),
                  pl.BlockSpec((N_META, POOL_WIDTH), lambda bb, i: (0, 0)),
                  pl.BlockSpec((N_POOL_GROUPS, POOL_GROUP_DIM, POOL_GROUP_DIM), lambda bb, i: (0, 0, 0)),
                  pl.BlockSpec((1, POOL_WIDTH), lambda bb, i: (0, 0))],
        out_specs=pl.BlockSpec((None, tm, POOL_WIDTH), lambda bb, i: (bb, i, 0)),
        out_shape=jax.ShapeDtypeStruct((b, s, POOL_WIDTH), bf16),
        scratch_shapes=[pltpu.VMEM((tm + N_META, POOL_WIDTH), f32)],
        compiler_params=_cparams(("parallel", "parallel")),
        name="pool_mixer",
    )(u3, u3, u_meta, pool_w, pool_scale)


def _order_key(score):
    bits = pltpu.bitcast(score + 0.0, i32)
    return jnp.where(bits < 0, bits ^ jnp.int32(0x7FFFFFFF), bits)


def _attn_kernel(q_ref, iq_ref, iw_ref, kT_ref, v_ref, ikT_ref, kTm_ref, vm_ref, ikTm_ref,
                 o_ref, key_ref, keym_ref, iwb_ref, s_ref, sm_ref, mrun_ref, lrun_ref, acc_ref,
                 *, n_sel, n_chunks, ord_bits):
    qb, kc, mc = q_ref.shape[0], kT_ref.shape[2], kTm_ref.shape[1]
    half = qb // 2
    sub = 2 * LANES
    meta = n_chunks
    qi = pl.program_id(1)
    n_ch = (qi * qb + qb - 1) // kc + 1
    row = qi * qb + lax.broadcasted_iota(i32, (qb, 1), 0)

    def tile_lanes(a, width):
        return jnp.concatenate([a] * (width // LANES), axis=1) if width > LANES else a

    def lane_max(a):
        out = a[:, :LANES]
        for j in range(1, a.shape[1] // LANES):
            out = jnp.maximum(out, a[:, j * LANES:(j + 1) * LANES])
        return out

    def lane_sum(a):
        out = a[:, :LANES]
        for j in range(1, a.shape[1] // LANES):
            out = out + a[:, j * LANES:(j + 1) * LANES]
        return out

    iq_heads = [iq_ref[:, h * IDX_DIM:(h + 1) * IDX_DIM] for h in range(N_IDX_HEADS)]
    iw = iw_ref[...]
    for h in range(N_IDX_HEADS):
        iwb_ref[h] = jnp.broadcast_to(iw[:, h:h + 1], (qb, LANES))

    def index_keys(ikT_c, valid):
        width = ikT_c.shape[1]
        acc = jnp.zeros((qb, width), f32)
        for h in range(N_IDX_HEADS):
            lg = jnp.dot(iq_heads[h], ikT_c, preferred_element_type=f32)
            acc = acc + jnp.maximum(lg, 0.0) * tile_lanes(iwb_ref[h], width)
        return jnp.where(valid, _order_key(acc), INT_MIN)

    def score_body(c, carry):
        for j in range(kc // sub):
            col = c * kc + j * sub + lax.broadcasted_iota(i32, (1, sub), 1)
            key_ref[c, :, j * sub:(j + 1) * sub] = index_keys(ikT_ref[c, :, j * sub:(j + 1) * sub], col <= row)
        return carry

    lax.fori_loop(0, n_ch, score_body, 0)
    keym_ref[...] = index_keys(ikTm_ref[...], lax.broadcasted_iota(i32, (1, mc), 1) < N_META)

    def keys_of(c, rows=slice(None)):
        return keym_ref[rows, :] if isinstance(c, int) and c == meta else key_ref[c, rows, :]

    def store_bias(c, bias):
        if isinstance(c, int) and c == meta:
            keym_ref[...] = pltpu.bitcast(bias, i32)
        else:
            key_ref[c] = pltpu.bitcast(bias, i32)

    def count_rows(pred_fn, operands):
        halves = [slice(r * half, (r + 1) * half) for r in range(2)]
        bcast = [[jnp.broadcast_to(o[rows], (half, LANES)) for o in operands] for rows in halves]
        accs = []
        for rows, ops in zip(halves, bcast):
            def add(c, acc, rows=rows, ops=ops):
                k = keys_of(c, rows)
                for j in range(k.shape[1] // LANES):
                    hit = pred_fn(c, j, k[:, j * LANES:(j + 1) * LANES], *ops)
                    acc = acc + jnp.where(hit, 1.0, 0.0)
                return acc
            accs.append(lax.fori_loop(0, n_ch, add, add(meta, jnp.zeros((half, LANES), f32))))
        return jnp.sum(jnp.concatenate(accs, axis=0), axis=1, keepdims=True)

    def count_ge(cand):
        return count_rows(lambda c, j, k, cb: k >= cb, [cand])

    def unsettled(n_ge):
        return jnp.max(jnp.where(n_ge != n_sel, 1.0, 0.0)) > 0.0

    c0 = count_ge(jnp.zeros((qb, 1), i32))
    t0 = jnp.where(c0 >= n_sel, 0, INT_MIN).astype(i32)
    n0 = jnp.where(c0 >= n_sel, c0, float(2 ** 24))

    def bisect(state):
        it, t, n_ge, _ = state
        cand = t + lax.shift_left(jnp.int32(1), 30 - it)
        cnt = count_ge(cand)
        ok = cnt >= n_sel
        n_ge = jnp.where(ok, cnt, n_ge)
        return it + 1, jnp.where(ok, cand, t), n_ge, unsettled(n_ge)

    _, thr, n_ge, _ = lax.while_loop(lambda st: jnp.logical_and(st[0] < 31, st[3]), bisect,
                                     (jnp.int32(0), t0, n0, unsettled(n0)))
    tied = jnp.logical_and(n_ge > n_sel, thr > INT_MIN)
    any_tied = jnp.max(jnp.where(tied, 1.0, 0.0)) > 0.0
    floor_key = jnp.maximum(thr, INT_MIN + 1)

    def ordinal(c, lanes):
        return jnp.where(c == meta, lanes, c * kc + lanes + N_META)

    @pl.when(jnp.logical_not(any_tied))
    def _():
        def to_bias(c, carry):
            store_bias(c, jnp.where(keys_of(c) >= floor_key, 0.0, NEG_BIG).astype(f32))
            return carry
        lax.fori_loop(0, n_ch, to_bias, 0)
        to_bias(meta, 0)

    @pl.when(any_tied)
    def _():
        need = n_sel - count_ge(thr + 1)
        lane_t = lax.broadcasted_iota(i32, (1, LANES), 1)

        def tie_pred(c, j, k, tb, cb):
            return jnp.logical_and(k == tb, ordinal(c, lane_t + j * LANES) <= cb)

        def bisect_pos(it, lo):
            cand = lo + lax.shift_left(jnp.int32(1), ord_bits - 1 - it)
            cnt = count_rows(tie_pred, [thr, cand])
            return jnp.where(cnt < need, cand, lo)

        lo = lax.fori_loop(0, ord_bits, bisect_pos, jnp.full((qb, 1), -1, i32))
        last = jnp.where(tied, lo + 1, jnp.int32(2 ** 30))

        def to_bias(c, carry):
            k = keys_of(c)
            lanes = lax.broadcasted_iota(i32, (1, k.shape[1]), 1)
            keep = jnp.logical_and(k >= floor_key, jnp.logical_or(k != thr, ordinal(c, lanes) <= last))
            store_bias(c, jnp.where(keep, 0.0, NEG_BIG).astype(f32))
            return carry
        lax.fori_loop(0, n_ch, to_bias, 0)
        to_bias(meta, 0)

    n_grp = s_ref.shape[0]
    low_half = lax.broadcasted_iota(i32, (1, 2 * HEAD_DIM), 1) < HEAD_DIM
    for grp in range(N_HEADS // n_grp):
        heads = [grp * n_grp + g for g in range(n_grp)]
        q_heads = [q_ref[:, h * HEAD_DIM:(h + 1) * HEAD_DIM] for h in heads]
        mrun_ref[...] = jnp.full(mrun_ref.shape, NEG_BIG, f32)

        def sweep_scores(c, carry):
            bias = pltpu.bitcast(key_ref[c], f32)
            for g, h in enumerate(heads):
                s = jnp.dot(q_heads[g], kT_ref[c, h * HEAD_DIM:(h + 1) * HEAD_DIM, :],
                            preferred_element_type=f32) + bias
                s_ref[g, c] = s
                mrun_ref[g] = jnp.maximum(mrun_ref[g], lane_max(s))
            return carry

        lax.fori_loop(0, n_ch, sweep_scores, 0)
        bias_m = pltpu.bitcast(keym_ref[...], f32)
        for g, h in enumerate(heads):
            s = jnp.dot(q_heads[g], kTm_ref[h * HEAD_DIM:(h + 1) * HEAD_DIM, :], preferred_element_type=f32) + bias_m
            sm_ref[g] = s
            row_max = jnp.max(jnp.maximum(mrun_ref[g], lane_max(s)), axis=1, keepdims=True)
            mrun_ref[g] = jnp.broadcast_to(row_max, (qb, LANES))

        lrun_ref[...] = jnp.zeros(lrun_ref.shape, f32)
        acc_ref[...] = jnp.zeros(acc_ref.shape, f32)

        def accumulate(get_s, v_rows, width):
            for hp in range(n_grp // 2):
                lanes = slice((heads[0] + 2 * hp) * HEAD_DIM, (heads[0] + 2 * hp + 2) * HEAD_DIM)
                v_pair = v_rows[:, lanes]
                parts = []
                for g in (2 * hp, 2 * hp + 1):
                    p = jnp.exp2(get_s(g) - tile_lanes(mrun_ref[g], width))
                    lrun_ref[g] = lrun_ref[g] + lane_sum(p)
                    parts.append(jnp.dot(p.astype(bf16), v_pair, preferred_element_type=f32))
                out = slice(2 * hp * HEAD_DIM, (2 * hp + 2) * HEAD_DIM)
                acc_ref[:, out] = acc_ref[:, out] + jnp.where(low_half, parts[0], parts[1])

        def sweep_values(c, carry):
            start = pl.multiple_of(c * kc, kc)
            accumulate(lambda g: s_ref[g, c], v_ref.at[pl.ds(start, kc)], kc)
            return carry

        lax.fori_loop(0, n_ch, sweep_values, 0)
        accumulate(lambda g: sm_ref[g], vm_ref, mc)
        for hp in range(n_grp // 2):
            out = slice(2 * hp * HEAD_DIM, (2 * hp + 2) * HEAD_DIM)
            denom = jnp.where(low_half, jnp.sum(lrun_ref[2 * hp], axis=1, keepdims=True),
                              jnp.sum(lrun_ref[2 * hp + 1], axis=1, keepdims=True))
            lanes = slice((heads[0] + 2 * hp) * HEAD_DIM, (heads[0] + 2 * hp + 2) * HEAD_DIM)
            o_ref[:, lanes] = (acc_ref[:, out] / denom).astype(o_ref.dtype)


def _attention(q, iq, iw, kT, v, ikT, kT_meta, v_meta, ikT_meta, batch, seq, n_sel):
    qb, kc, mc, grp = ATTN_QB, ATTN_KC, ATTN_MC, ATTN_GROUP
    n_q = seq // qb
    n_chunks = seq // kc
    ord_bits = max(1, (seq + N_META - 1).bit_length())
    kernel = functools.partial(_attn_kernel, n_sel=n_sel, n_chunks=n_chunks, ord_bits=ord_bits)
    once = pl.Buffered(1)
    return pl.pallas_call(
        kernel,
        grid=(batch, n_q),
        in_specs=[pl.BlockSpec((qb, ATTN_WIDTH), lambda b, i: (b * n_q + i, 0)),
                  pl.BlockSpec((qb, N_IDX_HEADS * IDX_DIM), lambda b, i: (b * n_q + i, 0)),
                  pl.BlockSpec((qb, LANES), lambda b, i: (b * n_q + i, 0)),
                  pl.BlockSpec((None, n_chunks, ATTN_WIDTH, kc), lambda b, i: (b, 0, 0, 0), pipeline_mode=once),
                  pl.BlockSpec((seq, ATTN_WIDTH), lambda b, i: (b, 0), pipeline_mode=once),
                  pl.BlockSpec((None, n_chunks, IDX_DIM, kc), lambda b, i: (b, 0, 0, 0), pipeline_mode=once),
                  pl.BlockSpec((ATTN_WIDTH, mc), lambda b, i: (0, 0)),
                  pl.BlockSpec((mc, ATTN_WIDTH), lambda b, i: (0, 0)),
                  pl.BlockSpec((IDX_DIM, mc), lambda b, i: (0, 0))],
        out_specs=pl.BlockSpec((qb, ATTN_WIDTH), lambda b, i: (b * n_q + i, 0)),
        out_shape=jax.ShapeDtypeStruct((batch * seq, ATTN_WIDTH), bf16),
        scratch_shapes=[pltpu.VMEM((n_chunks, qb, kc), i32),
                        pltpu.VMEM((qb, mc), i32),
                        pltpu.VMEM((N_IDX_HEADS, qb, LANES), f32),
                        pltpu.VMEM((grp, n_chunks, qb, kc), f32),
                        pltpu.VMEM((grp, qb, mc), f32),
                        pltpu.VMEM((grp, qb, LANES), f32),
                        pltpu.VMEM((grp, qb, LANES), f32),
                        pltpu.VMEM((qb, grp * HEAD_DIM), f32)],
        compiler_params=_cparams(("parallel", "arbitrary")),
        name="indexed_attention",
    )(q, iq, iw, kT, v, ikT, kT_meta, v_meta, ikT_meta)


def _post_kernel(x_ref, pp_ref, at_ref, sgp_ref, sga_ref, wbp_ref, wba_ref, wout_ref, g_ref, rw_ref, rb_ref,
                 h1_ref, hn_ref, lg_ref):
    y_pool = jnp.dot(pp_ref[...], wbp_ref[...], preferred_element_type=f32)
    y_attn = jnp.dot(at_ref[...], wba_ref[...], preferred_element_type=f32)
    merged = sgp_ref[...].astype(f32) * y_pool + sga_ref[...].astype(f32) * y_attn
    h1 = x_ref[...] + jnp.dot(merged.astype(bf16), wout_ref[...], preferred_element_type=f32)
    h1_ref[...] = h1
    hn = h1 * lax.rsqrt(jnp.mean(h1 * h1, axis=-1, keepdims=True) + EPS) * g_ref[...]
    hn_ref[...] = _pack_bf16_pairs(hn[:, :D_MODEL // 2], hn[:, D_MODEL // 2:])

    tm = hn.shape[0]
    hn_hi = hn.astype(bf16)
    hn_lo = (hn - hn_hi.astype(f32)).astype(bf16)
    r = jnp.dot(jnp.concatenate([hn_hi, hn_lo], axis=0), rw_ref[...], preferred_element_type=f32)
    top, bot = r[:tm], r[tm:]
    lg_ref[...] = top + pltpu.roll(top, LANES - N_EXPERTS, 1) + bot + rb_ref[...]


def _router_kernel(lg_ref, ri_ref, rg_ref, cnt_ref, carry_ref):
    tm = lg_ref.shape[0]

    @pl.when(pl.program_id(0) == 0)
    def _():
        carry_ref[...] = jnp.zeros_like(carry_ref)

    lane = lax.broadcasted_iota(i32, (tm, LANES), 1)
    logits = jnp.where(lane < N_EXPERTS, lg_ref[...], -jnp.inf)
    vals, idxs = [], []
    for _ in range(TOP_K):
        m = jnp.max(logits, axis=1, keepdims=True)
        idx = jnp.min(jnp.where(logits == m, lane, LANES), axis=1, keepdims=True)
        vals.append(m)
        idxs.append(idx)
        logits = jnp.where(lane == idx, -jnp.inf, logits)
    exps = [jnp.exp(v - vals[0]) for v in vals]
    denom = exps[0] + exps[1] + exps[2] + exps[3]

    hot = jnp.zeros((tm, LANES), f32)
    for idx in idxs:
        hot = hot + jnp.where(lane == idx, 1.0, 0.0)
    r_i = lax.broadcasted_iota(i32, (tm, tm), 0)
    c_i = lax.broadcasted_iota(i32, (tm, tm), 1)
    below = jnp.where(c_i < r_i, 1.0, 0.0).astype(bf16)
    before = jnp.dot(below, hot.astype(bf16), preferred_element_type=f32) + carry_ref[...]
    ri = jnp.zeros((tm, LANES), i32)
    rg = jnp.zeros((tm, LANES), f32)
    for k in range(TOP_K):
        rank = jnp.sum(jnp.where(lane == idxs[k], before, 0.0), axis=1, keepdims=True).astype(i32)
        ri = jnp.where(lane == k, idxs[k], ri)
        ri = jnp.where(lane == TOP_K + k, rank, ri)
        rg = jnp.where(lane == k, exps[k] / denom, rg)
    ri_ref[...] = ri
    rg_ref[...] = rg
    carry_ref[...] = carry_ref[...] + jnp.sum(hot, axis=0, keepdims=True)
    cnt_ref[...] = carry_ref[...]


def _post(x2d, pp, at, sgp, sga, wbp, wba, wout, g_moe, rw, rb, tm):
    n = x2d.shape[0]
    once = pl.Buffered(1)
    row = lambda w: pl.BlockSpec((tm, w), lambda i: (i, 0))
    full = lambda a, c: pl.BlockSpec((a, c), lambda i: (0, 0), pipeline_mode=once)
    return pl.pallas_call(
        _post_kernel,
        grid=(n // tm,),
        in_specs=[row(D_MODEL), row(POOL_WIDTH), row(ATTN_WIDTH), row(D_MODEL),
                  pl.BlockSpec((tm, D_MODEL), lambda i: (i, 1)),
                  full(POOL_WIDTH, D_MODEL), full(ATTN_WIDTH, D_MODEL), full(D_MODEL, D_MODEL),
                  full(1, D_MODEL), full(D_MODEL, LANES), full(1, LANES)],
        out_specs=[row(D_MODEL), row(D_MODEL // 2), row(LANES)],
        out_shape=[jax.ShapeDtypeStruct((n, D_MODEL), f32), jax.ShapeDtypeStruct((n, D_MODEL // 2), jnp.uint32),
                   jax.ShapeDtypeStruct((n, LANES), f32)],
        compiler_params=_cparams(("arbitrary",)),
        name="merge_outproj_logits",
    )(x2d, pp, at, sgp, sga, wbp, wba, wout, g_moe, rw, rb)


def _router(logits, tm):
    n = logits.shape[0]
    row = pl.BlockSpec((tm, LANES), lambda i: (i, 0))
    return pl.pallas_call(
        _router_kernel,
        grid=(n // tm,),
        in_specs=[row],
        out_specs=[row, row, pl.BlockSpec((1, LANES), lambda i: (0, 0))],
        out_shape=[jax.ShapeDtypeStruct((n, LANES), i32), jax.ShapeDtypeStruct((n, LANES), f32),
                   jax.ShapeDtypeStruct((1, LANES), f32)],
        scratch_shapes=[pltpu.VMEM((1, LANES), f32)],
        compiler_params=_cparams(("arbitrary",)),
        name="router_top4_rank",
    )(logits)


def _pack_bf16_pairs(hi, lo):
    hi_bits = pltpu.bitcast(hi.astype(jnp.bfloat16).astype(f32), jnp.uint32)
    lo_bits = pltpu.bitcast(lo.astype(jnp.bfloat16).astype(f32), jnp.uint32)
    return hi_bits | (lo_bits >> 16)


def _unpack_bf16_pairs(words):
    hi = pltpu.bitcast(words & jnp.uint32(0xFFFF0000), f32)
    lo = pltpu.bitcast(words << 16, f32)
    return hi, lo


def _expert_kernel(be_ref, nu_ref, bv_ref, tok_ref, tok_next_ref, hn_hbm, wg_ref, wu_ref, wd_ref, bg_ref, bu_ref, bd_ref,
                   y_ref, xbuf_ref, xb_ref, sems, *, n_blocks):
    b, f = pl.program_id(0), pl.program_id(1)
    bm = y_ref.shape[0]
    n_f = D_FF // wg_ref.shape[1]
    per_step = bm // n_f
    used = b < nu_ref[0]
    slot = b % 2

    def row_copy(tok, j, s):
        return pltpu.make_async_copy(hn_hbm.at[pl.ds(tok, 1)], xbuf_ref.at[s, pl.ds(j, 1)], sems.at[s])

    def wait_rows(s):
        pltpu.make_async_copy(xbuf_ref.at[s], xbuf_ref.at[s], sems.at[s]).wait()

    @pl.when(jnp.logical_and(b == 0, f == 0))
    def _():
        def issue(j, carry):
            row_copy(tok_ref[j], j, 0).start()
            return carry
        lax.fori_loop(0, bm, issue, 0)

    @pl.when(jnp.logical_and(f == 0, b <= nu_ref[0]))
    def _():
        wait_rows(slot)

    @pl.when(jnp.logical_and(used, f == 0))
    def _():
        hi, lo = _unpack_bf16_pairs(xbuf_ref[slot])
        xb_ref[:, :D_MODEL // 2] = hi.astype(bf16)
        xb_ref[:, D_MODEL // 2:] = lo.astype(bf16)
        y_ref[...] = jnp.broadcast_to(bd_ref[...], y_ref.shape)

    def ffn(rows):
        for j in range(per_step):
            r = f * per_step + j
            row_copy(tok_next_ref[r], r, 1 - slot).start()
        x = xb_ref[rows, :]
        gt = jnp.dot(x, wg_ref[...].astype(bf16), preferred_element_type=f32) + bg_ref[...]
        up = jnp.dot(x, wu_ref[...].astype(bf16), preferred_element_type=f32) + bu_ref[...]
        gt = jnp.minimum(gt, SWIGLU_LIMIT)
        up = jnp.clip(up, -SWIGLU_LIMIT, SWIGLU_LIMIT)
        hidden = (up + 1.0) * (gt * jax.nn.sigmoid(SWIGLU_ALPHA * gt))
        y_ref[rows, :] += jnp.dot(hidden.astype(bf16), wd_ref[...].astype(bf16), preferred_element_type=f32)

    half_full = bv_ref[b] <= bm // 2

    @pl.when(jnp.logical_and(used, jnp.logical_not(half_full)))
    def _():
        ffn(slice(None))

    @pl.when(jnp.logical_and(used, half_full))
    def _():
        ffn(slice(0, bm // 2))

    @pl.when(jnp.logical_and(used, jnp.logical_and(b == n_blocks - 1, f == n_f - 1)))
    def _():
        wait_rows(1 - slot)

    @pl.when(jnp.logical_and(jnp.logical_not(used), f == 0))
    def _():
        y_ref[...] = jnp.zeros_like(y_ref)


def _experts(block_expert, n_used, block_valid, row_tok, hn_words, wg, wu, wd, bg, bu, bd, bm, tf):
    n_rows = row_tok.shape[0]
    n_blocks = n_rows // bm
    n_f = D_FF // tf

    def blk(b, nu):
        return jnp.minimum(b, nu[0] - 1)

    def ftile(b, f, nu):
        return jnp.where(b < nu[0], f, n_f - 1)

    grid_spec = pltpu.PrefetchScalarGridSpec(
        num_scalar_prefetch=3,
        grid=(n_blocks, n_f),
        in_specs=[pl.BlockSpec((bm,), lambda b, f, be, nu, bv: (b,), memory_space=pltpu.SMEM),
                  pl.BlockSpec((bm,), lambda b, f, be, nu, bv: (jnp.minimum(b + 1, n_blocks - 1),),
                               memory_space=pltpu.SMEM),
                  pl.BlockSpec(memory_space=pl.ANY),
                  pl.BlockSpec((None, D_MODEL, tf), lambda b, f, be, nu, bv: (be[blk(b, nu)], 0, ftile(b, f, nu))),
                  pl.BlockSpec((None, D_MODEL, tf), lambda b, f, be, nu, bv: (be[blk(b, nu)], 0, ftile(b, f, nu))),
                  pl.BlockSpec((None, tf, D_MODEL), lambda b, f, be, nu, bv: (be[blk(b, nu)], ftile(b, f, nu), 0)),
                  pl.BlockSpec((None, 1, tf), lambda b, f, be, nu, bv: (be[blk(b, nu)], 0, ftile(b, f, nu))),
                  pl.BlockSpec((None, 1, tf), lambda b, f, be, nu, bv: (be[blk(b, nu)], 0, ftile(b, f, nu))),
                  pl.BlockSpec((None, 1, D_MODEL), lambda b, f, be, nu, bv: (be[blk(b, nu)], 0, 0))],
        out_specs=pl.BlockSpec((bm, D_MODEL), lambda b, f, be, nu, bv: (b, 0)),
        scratch_shapes=[pltpu.VMEM((2, bm, D_MODEL // 2), jnp.uint32),
                        pltpu.VMEM((bm, D_MODEL), bf16),
                        pltpu.SemaphoreType.DMA((2,))],
    )
    return pl.pallas_call(
        functools.partial(_expert_kernel, n_blocks=n_blocks),
        grid_spec=grid_spec,
        out_shape=jax.ShapeDtypeStruct((n_rows, D_MODEL), f32),
        compiler_params=_cparams(("arbitrary", "arbitrary")),
        name="moe_experts",
    )(block_expert, n_used, block_valid, row_tok, row_tok, hn_words, wg, wu, wd, bg, bu, bd)


def _combine_kernel(dest_ref, y_hbm, rg_ref, h1_ref, g_ref, o_ref, buf_ref, sem, *, tm):
    def row_copy(j, k):
        return pltpu.make_async_copy(y_hbm.at[pl.ds(dest_ref[j * TOP_K + k], 1)],
                                     buf_ref.at[k, pl.ds(j, 1)], sem)

    def issue(j, carry):
        for k in range(TOP_K):
            row_copy(j, k).start()
        return carry

    def drain(j, carry):
        for k in range(TOP_K):
            row_copy(j, k).wait()
        return carry

    lax.fori_loop(0, tm, issue, 0, unroll=8)
    lax.fori_loop(0, tm, drain, 0, unroll=8)
    rg = rg_ref[...]
    h = h1_ref[...]
    for k in range(TOP_K):
        h = h + buf_ref[k] * rg[:, k:k + 1]
    o_ref[...] = h * lax.rsqrt(jnp.mean(h * h, axis=-1, keepdims=True) + EPS) * g_ref[...]


def _combine(dest_flat, y, rg, h1, g_fin, tm):
    n = h1.shape[0]
    return pl.pallas_call(
        functools.partial(_combine_kernel, tm=tm),
        grid=(n // tm,),
        in_specs=[pl.BlockSpec((tm * TOP_K,), lambda i: (i,), memory_space=pltpu.SMEM),
                  pl.BlockSpec(memory_space=pl.ANY),
                  pl.BlockSpec((tm, LANES), lambda i: (i, 0)),
                  pl.BlockSpec((tm, D_MODEL), lambda i: (i, 0)),
                  pl.BlockSpec((1, D_MODEL), lambda i: (0, 0))],
        out_specs=pl.BlockSpec((tm, D_MODEL), lambda i: (i, 0)),
        out_shape=jax.ShapeDtypeStruct((n, D_MODEL), f32),
        scratch_shapes=[pltpu.VMEM((TOP_K, tm, D_MODEL), f32), pltpu.SemaphoreType.DMA],
        compiler_params=_cparams(("arbitrary",)),
        name="moe_combine_norm",
    )(dest_flat, y, rg, h1, g_fin)


def _rope_tables(n_pos):
    freqs = ROPE_THETA ** (-jnp.arange(0, ROT_DIM, 2, dtype=f32) / ROT_DIM)
    ang = jnp.arange(n_pos, dtype=f32)[:, None] * freqs[None, :]
    cos, sin = jnp.cos(ang), jnp.sin(ang)
    ones = jnp.ones((n_pos, HEAD_DIM - ROT_DIM), f32)
    zeros8 = jnp.zeros((n_pos, ROT_HALF), f32)
    zeros48 = jnp.zeros((n_pos, HEAD_DIM - ROT_DIM), f32)
    c = jnp.concatenate([cos, cos, ones], axis=1)
    s1 = jnp.concatenate([zeros8, sin, zeros48], axis=1)
    s2 = jnp.concatenate([-sin, zeros8, zeros48], axis=1)
    tile2 = lambda t: jnp.concatenate([t, t], axis=1)
    return (tile2(c), tile2(s1), tile2(s2)), (cos.T, sin.T)


def kernel(x, meta_tokens, norm_mix_g, w_in, pool_w, pool_scale, w_branch_pool, w_branch_attn, w_out,
           norm_moe_g, router_w, router_b, exp_w_gate, exp_b_gate, exp_w_up, exp_b_up, exp_w_down,
           exp_b_down, norm_final_g):
    batch, seq, _ = x.shape
    n_tok = batch * seq
    n_sel = min(TOPK_KEYS_MAX, (seq + N_META) // 4)
    x2d = x.reshape(n_tok, D_MODEL)
    kc, mc = ATTN_KC, ATTN_MC

    w = w_in[0]
    o_u, o_q, o_k, o_v, o_iq = 0, POOL_WIDTH, POOL_WIDTH + ATTN_WIDTH, POOL_WIDTH + 2 * ATTN_WIDTH, POOL_WIDTH + 3 * ATTN_WIDTH
    o_ik = o_iq + N_IDX_HEADS * IDX_DIM
    o_iw = o_ik + IDX_DIM
    o_gp = o_iw + N_IDX_HEADS
    o_ga = o_gp + D_MODEL
    cast = lambda a: a.astype(bf16)
    w_u, w_q, w_v, w_iq = (cast(w[:, o:o + 1024]) for o in (o_u, o_q, o_v, o_iq))
    w_kT = cast(w[:, o_k:o_k + ATTN_WIDTH].T)
    w_ikT = cast(w[:, o_ik:o_ik + IDX_DIM].T)
    w_iw = cast(jnp.pad(w[:, o_iw:o_iw + N_IDX_HEADS], ((0, 0), (0, LANES - N_IDX_HEADS)))
                * (N_IDX_HEADS ** -0.5 * IDX_DIM ** -0.5))
    w_gates = cast(w[:, o_gp:o_ga + D_MODEL])

    (tab_tok, (cosT, sinT)) = _rope_tables(seq + N_META)
    tok_x = tuple(t[N_META:] for t in tab_tok)
    cosT_x, sinT_x = cosT[:, N_META:], sinT[:, N_META:]
    pad_cols = lambda t, n: jnp.pad(t[:, :n] if t.shape[1] >= n else t, ((0, 0), (0, max(0, n - t.shape[1]))))
    cosT_m, sinT_m = pad_cols(cosT, mc), pad_cols(sinT, mc)

    xn = _rmsnorm(x2d, norm_mix_g[0], NORM_TM)
    meta_pad = jnp.pad(meta_tokens.astype(f32), ((0, mc - N_META), (0, 0)))
    xn_meta = _rmsnorm(meta_pad, norm_mix_g[0], mc)

    u = _proj(xn, w_u, PROJ_TM, f32)
    v = _proj(xn, w_v, PROJ_TM, bf16)
    sg = _proj(xn, w_gates, PROJ_TM, bf16, act="sigmoid")
    iw = _proj(xn, w_iw, PROJ_TM, f32)
    q = _proj_rope(xn, w_q, tok_x, PROJ_TM, seq // PROJ_TM, HEAD_DIM ** -0.5 * LOG2_E)
    iq = _proj_rope(xn, w_iq, tok_x, PROJ_TM, seq // PROJ_TM, 1.0)
    kT = _projT_rope(xn, w_kT, cosT_x, sinT_x, kc, seq // kc, N_HEADS)
    ikT = _projT_rope(xn, w_ikT, cosT_x, sinT_x, kc, seq // kc, 1)

    u_meta = _proj(xn_meta, w_u, mc, f32)[:N_META]
    v_meta = _proj(xn_meta, w_v, mc, bf16)
    kT_meta = _projT_rope(xn_meta, w_kT, cosT_m, sinT_m, mc, 1, N_HEADS)[0]
    ikT_meta = _projT_rope(xn_meta, w_ikT, cosT_m, sinT_m, mc, 1, 1)[0]

    pp = _pool(u.reshape(batch, seq, POOL_WIDTH), u_meta, cast(pool_w[0]), pool_scale[0].reshape(1, POOL_WIDTH),
               POOL_TM).reshape(n_tok, POOL_WIDTH)
    at = _attention(q, iq, iw, kT.reshape(batch, seq // kc, ATTN_WIDTH, kc), v,
                    ikT.reshape(batch, seq // kc, IDX_DIM, kc), kT_meta, v_meta, ikT_meta, batch, seq, n_sel)

    rw_f = router_w[0].astype(f32)
    rw_hi = rw_f.astype(bf16)
    rw_lo = (rw_f - rw_hi.astype(f32)).astype(bf16)
    rw = jnp.pad(jnp.concatenate([rw_hi, rw_lo], axis=1), ((0, 0), (0, LANES - 2 * N_EXPERTS)))
    rb = jnp.pad(router_b[0].astype(f32), (0, LANES - N_EXPERTS)).reshape(1, LANES)
    h1, hn, logits = _post(x2d, pp, at, sg, sg, cast(w_branch_pool[0]), cast(w_branch_attn[0]), cast(w_out[0]),
                           norm_moe_g[0].reshape(1, D_MODEL), rw, rb, POST_TM)
    ri, rg, cnt = _router(logits, ROUTER_TM)

    bm = MOE_BM
    counts = cnt[0, :N_EXPERTS].astype(i32)
    padded = ((counts + bm - 1) // bm) * bm
    pad_end = jnp.cumsum(padded)
    pad_start = pad_end - padded
    n_blocks = -(-(n_tok * TOP_K + N_EXPERTS * (bm - 1)) // bm)
    dest = (pad_start[ri[:, :TOP_K]] + ri[:, TOP_K:2 * TOP_K]).reshape(n_tok * TOP_K)
    block_expert = jnp.minimum(jnp.searchsorted(pad_end, jnp.arange(n_blocks, dtype=i32) * bm, side="right"),
                               N_EXPERTS - 1).astype(i32)
    n_used = (pad_end[-1] // bm).astype(i32).reshape(1)
    block_rows = jnp.arange(n_blocks, dtype=i32) * bm
    block_valid = jnp.clip(pad_start[block_expert] + counts[block_expert] - block_rows, 0, bm).astype(i32)

    row_tok = jnp.zeros((n_blocks * bm,), i32).at[dest].set(jnp.arange(n_tok * TOP_K, dtype=i32) // TOP_K)
    y = _experts(block_expert, n_used, block_valid, row_tok, hn, exp_w_gate[0], exp_w_up[0], exp_w_down[0],
                 exp_b_gate[0].reshape(N_EXPERTS, 1, D_FF), exp_b_up[0].reshape(N_EXPERTS, 1, D_FF),
                 exp_b_down[0].reshape(N_EXPERTS, 1, D_MODEL), bm, MOE_TF)
    out = _combine(dest, y, rg, h1, norm_final_g.reshape(1, D_MODEL), COMBINE_TM)
    return out.reshape(batch, seq, D_MODEL)
```

```python
import functools

import jax
import jax.numpy as jnp
from jax import lax
from jax.experimental import pallas as pl
from jax.experimental.pallas import tpu as pltpu

f32 = jnp.float32
bf16 = jnp.bfloat16
i32 = jnp.int32

D_MODEL = 2048
N_META = 16
POOL_WINDOWS = (2, 4, 8, 16)
N_POOL_GROUPS = 4
POOL_WIDTH = D_MODEL // 2
POOL_GROUP_DIM = POOL_WIDTH // N_POOL_GROUPS
N_HEADS = 16
HEAD_DIM = 64
ATTN_WIDTH = N_HEADS * HEAD_DIM
ROT_DIM = HEAD_DIM // 4
ROT_HALF = ROT_DIM // 2
ROPE_THETA = 500000.0
N_IDX_HEADS = 16
IDX_DIM = 64
TOPK_KEYS_MAX = 256
N_EXPERTS = 32
TOP_K = 4
D_FF = D_MODEL
SWIGLU_LIMIT = 7.0
SWIGLU_ALPHA = 1.702
EPS = 1e-5

LANES = 128
VMEM_LIMIT = 56 * 1024 * 1024

NORM_TM = 1024
PROJ_TM = 1024
ATTN_QB = 256
ATTN_KC = 512
ATTN_MC = 128
ATTN_GROUP = 2
POOL_TM = 1024
POST_TM = 512
ROUTER_TM = 1024
MOE_BM = 1024
MOE_TF = 256
MOE_WBUF = 3
COMBINE_TM = 512

LOG2_E = 1.4426950408889634
NEG_BIG = -1e30
INT_MIN = -(2 ** 31)


def _cparams(sem):
    return pltpu.CompilerParams(dimension_semantics=sem, vmem_limit_bytes=VMEM_LIMIT)


def _rmsnorm_kernel(x_ref, g_ref, o_ref):
    x = x_ref[...]
    ms = jnp.mean(x * x, axis=-1, keepdims=True)
    o_ref[...] = (x * lax.rsqrt(ms + EPS) * g_ref[...]).astype(o_ref.dtype)


def _rmsnorm(x2d, g, tm):
    n = x2d.shape[0]
    return pl.pallas_call(
        _rmsnorm_kernel,
        grid=(n // tm,),
        in_specs=[pl.BlockSpec((tm, D_MODEL), lambda i: (i, 0)),
                  pl.BlockSpec((1, D_MODEL), lambda i: (0, 0))],
        out_specs=pl.BlockSpec((tm, D_MODEL), lambda i: (i, 0)),
        out_shape=jax.ShapeDtypeStruct((n, D_MODEL), bf16),
        compiler_params=_cparams(("parallel",)),
        name="rmsnorm",
    )(x2d, g.reshape(1, D_MODEL))


def _proj_kernel(xn_ref, w_ref, o_ref, *, act):
    y = jnp.dot(xn_ref[...], w_ref[...], preferred_element_type=f32)
    if act == "sigmoid":
        y = jax.nn.sigmoid(y)
    o_ref[...] = y.astype(o_ref.dtype)


def _proj(xn, w, tm, out_dtype, act=None, tn=1024):
    n, width = xn.shape[0], w.shape[1]
    tn = min(tn, width)
    return pl.pallas_call(
        functools.partial(_proj_kernel, act=act),
        grid=(width // tn, n // tm),
        in_specs=[pl.BlockSpec((tm, D_MODEL), lambda j, i: (i, 0)),
                  pl.BlockSpec((D_MODEL, tn), lambda j, i: (0, j))],
        out_specs=pl.BlockSpec((tm, tn), lambda j, i: (i, j)),
        out_shape=jax.ShapeDtypeStruct((n, width), out_dtype),
        compiler_params=_cparams(("parallel", "parallel")),
        name="proj_" + (act or "plain"),
    )(xn, w)


def _proj_rope_kernel(xn_ref, w_ref, c_ref, s1_ref, s2_ref, o_ref, *, scale):
    y = jnp.dot(xn_ref[...], w_ref[...], preferred_element_type=f32)
    c, s1, s2 = c_ref[...], s1_ref[...], s2_ref[...]
    for j in range(y.shape[1] // LANES):
        yt = y[:, j * LANES:(j + 1) * LANES]
        r = yt * c + pltpu.roll(yt, ROT_HALF, 1) * s1 + pltpu.roll(yt, LANES - ROT_HALF, 1) * s2
        o_ref[:, j * LANES:(j + 1) * LANES] = (r * scale).astype(o_ref.dtype)


def _proj_rope(xn, w, tables, tm, pos_blocks, scale):
    n, width = xn.shape[0], w.shape[1]
    tspec = pl.BlockSpec((tm, LANES), lambda i: (i % pos_blocks, 0))
    return pl.pallas_call(
        functools.partial(_proj_rope_kernel, scale=scale),
        grid=(n // tm,),
        in_specs=[pl.BlockSpec((tm, D_MODEL), lambda i: (i, 0)),
                  pl.BlockSpec((D_MODEL, width), lambda i: (0, 0)),
                  tspec, tspec, tspec],
        out_specs=pl.BlockSpec((tm, width), lambda i: (i, 0)),
        out_shape=jax.ShapeDtypeStruct((n, width), bf16),
        compiler_params=_cparams(("parallel",)),
        name="proj_rope",
    )(xn, w, *tables)


def _projT_rope_kernel(wT_ref, xn_ref, cos_ref, sin_ref, o_ref, *, n_heads):
    y = lax.dot_general(wT_ref[...], xn_ref[...], (((1,), (1,)), ((), ())),
                        preferred_element_type=f32)
    c, s = cos_ref[...], sin_ref[...]
    for h in range(n_heads):
        blk = y[h * HEAD_DIM:(h + 1) * HEAD_DIM]
        x1, x2 = blk[0:ROT_HALF], blk[ROT_HALF:ROT_DIM]
        out = jnp.concatenate([x1 * c - x2 * s, x2 * c + x1 * s, blk[ROT_DIM:]], axis=0)
        o_ref[h * HEAD_DIM:(h + 1) * HEAD_DIM, :] = out.astype(o_ref.dtype)


def _projT_rope(xn, wT, cosT, sinT, tm, pos_blocks, n_heads):
    n, rows = xn.shape[0], wT.shape[0]
    tspec = pl.BlockSpec((ROT_HALF, tm), lambda i: (0, i % pos_blocks))
    return pl.pallas_call(
        functools.partial(_projT_rope_kernel, n_heads=n_heads),
        grid=(n // tm,),
        in_specs=[pl.BlockSpec((rows, D_MODEL), lambda i: (0, 0)),
                  pl.BlockSpec((tm, D_MODEL), lambda i: (i, 0)),
                  tspec, tspec],
        out_specs=pl.BlockSpec((None, rows, tm), lambda i: (i, 0, 0)),
        out_shape=jax.ShapeDtypeStruct((n // tm, rows, tm), bf16),
        compiler_params=_cparams(("parallel",)),
        name="projT_rope",
    )(wT, xn, cosT, sinT)


def _pool_kernel(u_ref, prev_ref, meta_ref, pw_ref, ps_ref, o_ref, ext_ref, *, tm):
    i = pl.program_id(1)
    halo = jnp.where(i == 0, meta_ref[...], prev_ref[...])
    ext_ref[0:N_META, :] = halo
    ext_ref[N_META:, :] = u_ref[...]
    pos = i * tm + lax.broadcasted_iota(i32, (tm, 1), 0) + N_META
    for g, w in enumerate(POOL_WINDOWS):
        cols = slice(g * POOL_GROUP_DIM, (g + 1) * POOL_GROUP_DIM)
        acc = ext_ref[N_META:, cols]
        for j in range(1, w):
            acc = acc + ext_ref[pl.ds(N_META - j, tm), cols]
        count = jnp.minimum(pos + 1, w).astype(f32)
        pooled = acc / count - ext_ref[N_META:, cols]
        mixed = jnp.dot(pooled.astype(bf16), pw_ref[g], preferred_element_type=f32)
        o_ref[:, cols] = (mixed * ps_ref[:, cols]).astype(o_ref.dtype)


def _pool(u3, u_meta, pool_w, pool_scale, tm):
    b, s, _ = u3.shape
    per = tm // N_META
    return pl.pallas_call(
        functools.partial(_pool_kernel, tm=tm),
        grid=(b, s // tm),
        in_specs=[pl.BlockSpec((None, tm, POOL_WIDTH), lambda bb, i: (bb, i, 0)),
                  pl.BlockSpec((None, N_META, POOL_WIDTH),
                               lambda bb, i: (bb, jnp.maximum(i * per - 1, 0), 0)),
                  pl.BlockSpec((N_META, POOL_WIDTH), lambda bb, i: (0, 0)),
                  pl.BlockSpec((N_POOL_GROUPS, POOL_GROUP_DIM, POOL_GROUP_DIM), lambda bb, i: (0, 0, 0)),
                  pl.BlockSpec((1, POOL_WIDTH), lambda bb, i: (0, 0))],
        out_specs=pl.BlockSpec((None, tm, POOL_WIDTH), lambda bb, i: (bb, i, 0)),
        out_shape=jax.ShapeDtypeStruct((b, s, POOL_WIDTH), bf16),
        scratch_shapes=[pltpu.VMEM((tm + N_META, POOL_WIDTH), f32)],
        compiler_params=_cparams(("parallel", "parallel")),
        name="pool_mixer",
    )(u3, u3, u_meta, pool_w, pool_scale)


def _order_key(score):
    bits = pltpu.bitcast(score + 0.0, i32)
    return jnp.where(bits < 0, bits ^ jnp.int32(0x7FFFFFFF), bits)


def _attn_kernel(q_ref, iq_ref, iw_ref, kT_ref, v_ref, ikT_ref, kTm_ref, vm_ref, ikTm_ref,
                 o_ref, key_ref, keym_ref, iwb_ref, s_ref, sm_ref, mrun_ref, lrun_ref, acc_ref,
                 *, n_sel, n_chunks, ord_bits):
    qb, kc, mc = q_ref.shape[0], kT_ref.shape[2], kTm_ref.shape[1]
    half = qb // 2
    sub = 2 * LANES
    meta = n_chunks
    qi = pl.program_id(1)
    n_ch = (qi * qb + qb - 1) // kc + 1
    row = qi * qb + lax.broadcasted_iota(i32, (qb, 1), 0)

    def tile_lanes(a, width):
        return jnp.concatenate([a] * (width // LANES), axis=1) if width > LANES else a

    def lane_max(a):
        out = a[:, :LANES]
        for j in range(1, a.shape[1] // LANES):
            out = jnp.maximum(out, a[:, j * LANES:(j + 1) * LANES])
        return out

    def lane_sum(a):
        out = a[:, :LANES]
        for j in range(1, a.shape[1] // LANES):
            out = out + a[:, j * LANES:(j + 1) * LANES]
        return out

    iq_heads = [iq_ref[:, h * IDX_DIM:(h + 1) * IDX_DIM] for h in range(N_IDX_HEADS)]
    iw = iw_ref[...]
    for h in range(N_IDX_HEADS):
        iwb_ref[h] = jnp.broadcast_to(iw[:, h:h + 1], (qb, LANES))

    def index_keys(ikT_c, valid):
        width = ikT_c.shape[1]
        acc = jnp.zeros((qb, width), f32)
        for h in range(N_IDX_HEADS):
            lg = jnp.dot(iq_heads[h], ikT_c, preferred_element_type=f32)
            acc = acc + jnp.maximum(lg, 0.0) * tile_lanes(iwb_ref[h], width)
        return jnp.where(valid, _order_key(acc), INT_MIN)

    def score_body(c, carry):
        for j in range(kc // sub):
            col = c * kc + j * sub + lax.broadcasted_iota(i32, (1, sub), 1)
            key_ref[c, :, j * sub:(j + 1) * sub] = index_keys(ikT_ref[c, :, j * sub:(j + 1) * sub], col <= row)
        return carry

    def chunk_loop(body, carry=0):
        carry = lax.fori_loop(0, n_ch // 2, lambda i, cr: body(2 * i + 1, body(2 * i, cr)), carry)
        return lax.fori_loop(n_ch - n_ch % 2, n_ch, body, carry)

    chunk_loop(score_body)
    keym_ref[...] = index_keys(ikTm_ref[...], lax.broadcasted_iota(i32, (1, mc), 1) < N_META)

    def keys_of(c, rows=slice(None)):
        return keym_ref[rows, :] if isinstance(c, int) and c == meta else key_ref[c, rows, :]

    def store_bias(c, bias):
        if isinstance(c, int) and c == meta:
            keym_ref[...] = pltpu.bitcast(bias, i32)
        else:
            key_ref[c] = pltpu.bitcast(bias, i32)

    def count_rows(pred_fn, operands):
        halves = [slice(r * half, (r + 1) * half) for r in range(2)]
        bcast = [[jnp.broadcast_to(o[rows], (half, LANES)) for o in operands] for rows in halves]
        accs = []
        for rows, ops in zip(halves, bcast):
            def add(c, acc, rows=rows, ops=ops):
                k = keys_of(c, rows)
                for j in range(k.shape[1] // LANES):
                    hit = pred_fn(c, j, k[:, j * LANES:(j + 1) * LANES], *ops)
                    acc = acc + jnp.where(hit, 1.0, 0.0)
                return acc
            accs.append(chunk_loop(add, add(meta, jnp.zeros((half, LANES), f32))))
        return jnp.sum(jnp.concatenate(accs, axis=0), axis=1, keepdims=True)

    def count_ge(cand):
        return count_rows(lambda c, j, k, cb: k >= cb, [cand])

    def unsettled(n_ge):
        return jnp.max(jnp.where(n_ge != n_sel, 1.0, 0.0)) > 0.0

    c0 = count_ge(jnp.zeros((qb, 1), i32))
    t0 = jnp.where(c0 >= n_sel, 0, INT_MIN).astype(i32)
    n0 = jnp.where(c0 >= n_sel, c0, float(2 ** 24))

    def bisect(state):
        it, t, n_ge, _ = state
        cand = t + lax.shift_left(jnp.int32(1), 30 - it)
        cnt = count_ge(cand)
        ok = cnt >= n_sel
        n_ge = jnp.where(ok, cnt, n_ge)
        return it + 1, jnp.where(ok, cand, t), n_ge, unsettled(n_ge)

    _, thr, n_ge, _ = lax.while_loop(lambda st: jnp.logical_and(st[0] < 31, st[3]), bisect,
                                     (jnp.int32(0), t0, n0, unsettled(n0)))
    tied = jnp.logical_and(n_ge > n_sel, thr > INT_MIN)
    any_tied = jnp.max(jnp.where(tied, 1.0, 0.0)) > 0.0
    floor_key = jnp.maximum(thr, INT_MIN + 1)

    def ordinal(c, lanes):
        return jnp.where(c == meta, lanes, c * kc + lanes + N_META)

    @pl.when(jnp.logical_not(any_tied))
    def _():
        def to_bias(c, carry):
            store_bias(c, jnp.where(keys_of(c) >= floor_key, 0.0, NEG_BIG).astype(f32))
            return carry
        lax.fori_loop(0, n_ch, to_bias, 0)
        to_bias(meta, 0)

    @pl.when(any_tied)
    def _():
        need = n_sel - count_ge(thr + 1)
        lane_t = lax.broadcasted_iota(i32, (1, LANES), 1)

        def tie_pred(c, j, k, tb, cb):
            return jnp.logical_and(k == tb, ordinal(c, lane_t + j * LANES) <= cb)

        def bisect_pos(it, lo):
            cand = lo + lax.shift_left(jnp.int32(1), ord_bits - 1 - it)
            cnt = count_rows(tie_pred, [thr, cand])
            return jnp.where(cnt < need, cand, lo)

        lo = lax.fori_loop(0, ord_bits, bisect_pos, jnp.full((qb, 1), -1, i32))
        last = jnp.where(tied, lo + 1, jnp.int32(2 ** 30))

        def to_bias(c, carry):
            k = keys_of(c)
            lanes = lax.broadcasted_iota(i32, (1, k.shape[1]), 1)
            keep = jnp.logical_and(k >= floor_key, jnp.logical_or(k != thr, ordinal(c, lanes) <= last))
            store_bias(c, jnp.where(keep, 0.0, NEG_BIG).astype(f32))
            return carry
        lax.fori_loop(0, n_ch, to_bias, 0)
        to_bias(meta, 0)

    n_grp = s_ref.shape[1]
    n_groups = N_HEADS // n_grp
    low_half = lax.broadcasted_iota(i32, (1, 2 * HEAD_DIM), 1) < HEAD_DIM

    def heads_of(grp):
        return [grp * n_grp + g for g in range(n_grp)]

    def q_of(grp):
        return [q_ref[:, h * HEAD_DIM:(h + 1) * HEAD_DIM] for h in heads_of(grp)]

    def scores_chunk(grp, q_heads, c):
        slot = grp % 2
        bias = pltpu.bitcast(key_ref[c], f32)
        for g, h in enumerate(heads_of(grp)):
            s = jnp.dot(q_heads[g], kT_ref[c, h * HEAD_DIM:(h + 1) * HEAD_DIM, :],
                        preferred_element_type=f32) + bias
            s_ref[slot, g, c] = s
            mrun_ref[slot, g] = jnp.maximum(mrun_ref[slot, g], lane_max(s))

    def scores_meta(grp, q_heads):
        slot = grp % 2
        bias_m = pltpu.bitcast(keym_ref[...], f32)
        for g, h in enumerate(heads_of(grp)):
            s = jnp.dot(q_heads[g], kTm_ref[h * HEAD_DIM:(h + 1) * HEAD_DIM, :], preferred_element_type=f32) + bias_m
            sm_ref[slot, g] = s
            row_max = jnp.max(jnp.maximum(mrun_ref[slot, g], lane_max(s)), axis=1, keepdims=True)
            mrun_ref[slot, g] = jnp.broadcast_to(row_max, (qb, LANES))

    def accumulate(grp, get_s, v_rows, width):
        slot = grp % 2
        first = heads_of(grp)[0]
        for hp in range(n_grp // 2):
            lanes = slice((first + 2 * hp) * HEAD_DIM, (first + 2 * hp + 2) * HEAD_DIM)
            v_pair = v_rows[:, lanes]
            parts = []
            for g in (2 * hp, 2 * hp + 1):
                p = jnp.exp2(get_s(g) - tile_lanes(mrun_ref[slot, g], width))
                lrun_ref[g] = lrun_ref[g] + lane_sum(p)
                parts.append(jnp.dot(p.astype(bf16), v_pair, preferred_element_type=f32))
            out = slice(2 * hp * HEAD_DIM, (2 * hp + 2) * HEAD_DIM)
            acc_ref[:, out] = acc_ref[:, out] + jnp.where(low_half, parts[0], parts[1])

    def values_chunk(grp, c):
        start = pl.multiple_of(c * kc, kc)
        accumulate(grp, lambda g: s_ref[grp % 2, g, c], v_ref.at[pl.ds(start, kc)], kc)

    def finish(grp):
        accumulate(grp, lambda g: sm_ref[grp % 2, g], vm_ref, mc)
        first = heads_of(grp)[0]
        for hp in range(n_grp // 2):
            out = slice(2 * hp * HEAD_DIM, (2 * hp + 2) * HEAD_DIM)
            denom = jnp.where(low_half, jnp.sum(lrun_ref[2 * hp], axis=1, keepdims=True),
                              jnp.sum(lrun_ref[2 * hp + 1], axis=1, keepdims=True))
            lanes = slice((first + 2 * hp) * HEAD_DIM, (first + 2 * hp + 2) * HEAD_DIM)
            o_ref[:, lanes] = (acc_ref[:, out] / denom).astype(o_ref.dtype)

    def reset_max(grp):
        mrun_ref[grp % 2] = jnp.full(mrun_ref.shape[1:], NEG_BIG, f32)

    q_now = q_of(0)
    reset_max(0)
    chunk_loop(lambda c, carry: (scores_chunk(0, q_now, c), carry)[1])
    scores_meta(0, q_now)
    for grp in range(n_groups):
        lrun_ref[...] = jnp.zeros(lrun_ref.shape, f32)
        acc_ref[...] = jnp.zeros(acc_ref.shape, f32)
        if grp + 1 < n_groups:
            q_next = q_of(grp + 1)
            reset_max(grp + 1)

            def both(c, carry, grp=grp, q_next=q_next):
                values_chunk(grp, c)
                scores_chunk(grp + 1, q_next, c)
                return carry

            chunk_loop(both)
            finish(grp)
            scores_meta(grp + 1, q_next)
        else:
            chunk_loop(lambda c, carry, grp=grp: (values_chunk(grp, c), carry)[1])
            finish(grp)


def _attention(q, iq, iw, kT, v, ikT, kT_meta, v_meta, ikT_meta, batch, seq, n_sel):
    qb, kc, mc, grp = ATTN_QB, ATTN_KC, ATTN_MC, ATTN_GROUP
    n_q = seq // qb
    n_chunks = seq // kc
    ord_bits = max(1, (seq + N_META - 1).bit_length())
    kernel = functools.partial(_attn_kernel, n_sel=n_sel, n_chunks=n_chunks, ord_bits=ord_bits)
    once = pl.Buffered(1)
    return pl.pallas_call(
        kernel,
        grid=(batch, n_q),
        in_specs=[pl.BlockSpec((qb, ATTN_WIDTH), lambda b, i: (b * n_q + i, 0)),
                  pl.BlockSpec((qb, N_IDX_HEADS * IDX_DIM), lambda b, i: (b * n_q + i, 0)),
                  pl.BlockSpec((qb, LANES), lambda b, i: (b * n_q + i, 0)),
                  pl.BlockSpec((None, n_chunks, ATTN_WIDTH, kc), lambda b, i: (b, 0, 0, 0), pipeline_mode=once),
                  pl.BlockSpec((seq, ATTN_WIDTH), lambda b, i: (b, 0), pipeline_mode=once),
                  pl.BlockSpec((None, n_chunks, IDX_DIM, kc), lambda b, i: (b, 0, 0, 0), pipeline_mode=once),
                  pl.BlockSpec((ATTN_WIDTH, mc), lambda b, i: (0, 0)),
                  pl.BlockSpec((mc, ATTN_WIDTH), lambda b, i: (0, 0)),
                  pl.BlockSpec((IDX_DIM, mc), lambda b, i: (0, 0))],
        out_specs=pl.BlockSpec((qb, ATTN_WIDTH), lambda b, i: (b * n_q + i, 0)),
        out_shape=jax.ShapeDtypeStruct((batch * seq, ATTN_WIDTH), bf16),
        scratch_shapes=[pltpu.VMEM((n_chunks, qb, kc), i32),
                        pltpu.VMEM((qb, mc), i32),
                        pltpu.VMEM((N_IDX_HEADS, qb, LANES), f32),
                        pltpu.VMEM((2, grp, n_chunks, qb, kc), f32),
                        pltpu.VMEM((2, grp, qb, mc), f32),
                        pltpu.VMEM((2, grp, qb, LANES), f32),
                        pltpu.VMEM((grp, qb, LANES), f32),
                        pltpu.VMEM((qb, grp * HEAD_DIM), f32)],
        compiler_params=_cparams(("parallel", "arbitrary")),
        name="indexed_attention",
    )(q, iq, iw, kT, v, ikT, kT_meta, v_meta, ikT_meta)


def _post_kernel(x_ref, pp_ref, at_ref, sgp_ref, sga_ref, wbp_ref, wba_ref, wout_ref, g_ref, rw_ref, rb_ref,
                 h1_ref, hn_ref, lg_ref):
    y_pool = jnp.dot(pp_ref[...], wbp_ref[...], preferred_element_type=f32)
    y_attn = jnp.dot(at_ref[...], wba_ref[...], preferred_element_type=f32)
    merged = sgp_ref[...].astype(f32) * y_pool + sga_ref[...].astype(f32) * y_attn
    h1 = x_ref[...] + jnp.dot(merged.astype(bf16), wout_ref[...], preferred_element_type=f32)
    h1_ref[...] = h1
    hn = h1 * lax.rsqrt(jnp.mean(h1 * h1, axis=-1, keepdims=True) + EPS) * g_ref[...]
    hn_ref[...] = _pack_bf16_pairs(hn[:, :D_MODEL // 2], hn[:, D_MODEL // 2:])

    tm = hn.shape[0]
    hn_hi = hn.astype(bf16)
    hn_lo = (hn - hn_hi.astype(f32)).astype(bf16)
    r = jnp.dot(jnp.concatenate([hn_hi, hn_lo], axis=0), rw_ref[...], preferred_element_type=f32)
    top, bot = r[:tm], r[tm:]
    lg_ref[...] = top + pltpu.roll(top, LANES - N_EXPERTS, 1) + bot + rb_ref[...]


def _router_kernel(lg_ref, ri_ref, rg_ref, cnt_ref, carry_ref):
    tm = lg_ref.shape[0]

    @pl.when(pl.program_id(0) == 0)
    def _():
        carry_ref[...] = jnp.zeros_like(carry_ref)

    lane = lax.broadcasted_iota(i32, (tm, LANES), 1)
    logits = jnp.where(lane < N_EXPERTS, lg_ref[...], -jnp.inf)
    vals, idxs = [], []
    for _ in range(TOP_K):
        m = jnp.max(logits, axis=1, keepdims=True)
        idx = jnp.min(jnp.where(logits == m, lane, LANES), axis=1, keepdims=True)
        vals.append(m)
        idxs.append(idx)
        logits = jnp.where(lane == idx, -jnp.inf, logits)
    exps = [jnp.exp(v - vals[0]) for v in vals]
    denom = exps[0] + exps[1] + exps[2] + exps[3]

    hot = jnp.zeros((tm, LANES), f32)
    for idx in idxs:
        hot = hot + jnp.where(lane == idx, 1.0, 0.0)
    r_i = lax.broadcasted_iota(i32, (tm, tm), 0)
    c_i = lax.broadcasted_iota(i32, (tm, tm), 1)
    below = jnp.where(c_i < r_i, 1.0, 0.0).astype(bf16)
    before = jnp.dot(below, hot.astype(bf16), preferred_element_type=f32) + carry_ref[...]
    ri = jnp.zeros((tm, LANES), i32)
    rg = jnp.zeros((tm, LANES), f32)
    for k in range(TOP_K):
        rank = jnp.sum(jnp.where(lane == idxs[k], before, 0.0), axis=1, keepdims=True).astype(i32)
        ri = jnp.where(lane == k, idxs[k], ri)
        ri = jnp.where(lane == TOP_K + k, rank, ri)
        rg = jnp.where(lane == k, exps[k] / denom, rg)
    ri_ref[...] = ri
    rg_ref[...] = rg
    carry_ref[...] = carry_ref[...] + jnp.sum(hot, axis=0, keepdims=True)
    cnt_ref[...] = carry_ref[...]


def _post(x2d, pp, at, sgp, sga, wbp, wba, wout, g_moe, rw, rb, tm):
    n = x2d.shape[0]
    once = pl.Buffered(1)
    row = lambda w: pl.BlockSpec((tm, w), lambda i: (i, 0))
    full = lambda a, c: pl.BlockSpec((a, c), lambda i: (0, 0), pipeline_mode=once)
    return pl.pallas_call(
        _post_kernel,
        grid=(n // tm,),
        in_specs=[row(D_MODEL), row(POOL_WIDTH), row(ATTN_WIDTH), row(D_MODEL),
                  pl.BlockSpec((tm, D_MODEL), lambda i: (i, 1)),
                  full(POOL_WIDTH, D_MODEL), full(ATTN_WIDTH, D_MODEL), full(D_MODEL, D_MODEL),
                  full(1, D_MODEL), full(D_MODEL, LANES), full(1, LANES)],
        out_specs=[row(D_MODEL), row(D_MODEL // 2), row(LANES)],
        out_shape=[jax.ShapeDtypeStruct((n, D_MODEL), f32), jax.ShapeDtypeStruct((n, D_MODEL // 2), jnp.uint32),
                   jax.ShapeDtypeStruct((n, LANES), f32)],
        compiler_params=_cparams(("arbitrary",)),
        name="merge_outproj_logits",
    )(x2d, pp, at, sgp, sga, wbp, wba, wout, g_moe, rw, rb)


def _router(logits, tm):
    n = logits.shape[0]
    row = pl.BlockSpec((tm, LANES), lambda i: (i, 0))
    return pl.pallas_call(
        _router_kernel,
        grid=(n // tm,),
        in_specs=[row],
        out_specs=[row, row, pl.BlockSpec((1, LANES), lambda i: (0, 0))],
        out_shape=[jax.ShapeDtypeStruct((n, LANES), i32), jax.ShapeDtypeStruct((n, LANES), f32),
                   jax.ShapeDtypeStruct((1, LANES), f32)],
        scratch_shapes=[pltpu.VMEM((1, LANES), f32)],
        compiler_params=_cparams(("arbitrary",)),
        name="router_top4_rank",
    )(logits)


def _pack_bf16_pairs(hi, lo):
    hi_bits = pltpu.bitcast(hi.astype(jnp.bfloat16).astype(f32), jnp.uint32)
    lo_bits = pltpu.bitcast(lo.astype(jnp.bfloat16).astype(f32), jnp.uint32)
    return hi_bits | (lo_bits >> 16)


def _unpack_bf16_pairs(words):
    hi = pltpu.bitcast(words & jnp.uint32(0xFFFF0000), f32)
    lo = pltpu.bitcast(words << 16, f32)
    return hi, lo


def _expert_kernel(be_ref, nu_ref, bv_ref, tok_ref, tok_next_ref, hn_hbm, wg_hbm, wu_hbm, wd_hbm, bg_ref, bu_ref, bd_ref,
                   y_ref, xbuf_ref, xb_ref, wg_buf, wu_buf, wd_buf, sems, wsems, *, n_blocks):
    b, f = pl.program_id(0), pl.program_id(1)
    bm = y_ref.shape[0]
    tf = wg_buf.shape[2]
    n_f = D_FF // tf
    per_step = bm // n_f
    used = b < nu_ref[0]
    slot = b % 2
    n_wbuf = wg_buf.shape[0]
    step = b * n_f + f
    wslot = step % n_wbuf

    def weight_copies(s):
        e = be_ref[s // n_f]
        cols = pl.ds(pl.multiple_of((s % n_f) * tf, tf), tf)
        k = s % n_wbuf
        return (pltpu.make_async_copy(wg_hbm.at[e].at[:, cols], wg_buf.at[k], wsems.at[k]),
                pltpu.make_async_copy(wu_hbm.at[e].at[:, cols], wu_buf.at[k], wsems.at[k]),
                pltpu.make_async_copy(wd_hbm.at[e].at[cols, :], wd_buf.at[k], wsems.at[k]))

    @pl.when(used)
    def _():
        total = nu_ref[0] * n_f

        @pl.when(step == 0)
        def _():
            for ahead in range(n_wbuf):
                @pl.when(ahead < total)
                def _():
                    for c in weight_copies(ahead):
                        c.start()

        @pl.when(jnp.logical_and(step > 0, step + n_wbuf - 1 < total))
        def _():
            for c in weight_copies(step + n_wbuf - 1):
                c.start()

        for c in weight_copies(step):
            c.wait()

    def row_copy(tok, j, s):
        return pltpu.make_async_copy(hn_hbm.at[pl.ds(tok, 1)], xbuf_ref.at[s, pl.ds(j, 1)], sems.at[s])

    def wait_rows(s):
        pltpu.make_async_copy(xbuf_ref.at[s], xbuf_ref.at[s], sems.at[s]).wait()

    @pl.when(jnp.logical_and(b == 0, f == 0))
    def _():
        def issue(j, carry):
            row_copy(tok_ref[j], j, 0).start()
            return carry
        lax.fori_loop(0, bm, issue, 0)

    @pl.when(jnp.logical_and(f == 0, b <= nu_ref[0]))
    def _():
        wait_rows(slot)

    @pl.when(jnp.logical_and(used, f == 0))
    def _():
        hi, lo = _unpack_bf16_pairs(xbuf_ref[slot])
        xb_ref[:, :D_MODEL // 2] = hi.astype(bf16)
        xb_ref[:, D_MODEL // 2:] = lo.astype(bf16)
        y_ref[...] = jnp.broadcast_to(bd_ref[...], y_ref.shape)

    def ffn(rows):
        for j in range(per_step):
            r = f * per_step + j
            row_copy(tok_next_ref[r], r, 1 - slot).start()
        x = xb_ref[rows, :]
        gt = jnp.dot(x, wg_buf[wslot].astype(bf16), preferred_element_type=f32) + bg_ref[...]
        up = jnp.dot(x, wu_buf[wslot].astype(bf16), preferred_element_type=f32) + bu_ref[...]
        gt = jnp.minimum(gt, SWIGLU_LIMIT)
        up = jnp.clip(up, -SWIGLU_LIMIT, SWIGLU_LIMIT)
        hidden = (up + 1.0) * (gt * jax.nn.sigmoid(SWIGLU_ALPHA * gt))
        y_ref[rows, :] += jnp.dot(hidden.astype(bf16), wd_buf[wslot].astype(bf16), preferred_element_type=f32)

    half_full = bv_ref[b] <= bm // 2

    @pl.when(jnp.logical_and(used, jnp.logical_not(half_full)))
    def _():
        ffn(slice(None))

    @pl.when(jnp.logical_and(used, half_full))
    def _():
        ffn(slice(0, bm // 2))

    @pl.when(jnp.logical_and(used, jnp.logical_and(b == n_blocks - 1, f == n_f - 1)))
    def _():
        wait_rows(1 - slot)

    @pl.when(jnp.logical_and(jnp.logical_not(used), f == 0))
    def _():
        y_ref[...] = jnp.zeros_like(y_ref)


def _experts(block_expert, n_used, block_valid, row_tok, hn_words, wg, wu, wd, bg, bu, bd, bm, tf):
    n_rows = row_tok.shape[0]
    n_blocks = n_rows // bm
    n_f = D_FF // tf

    def blk(b, nu):
        return jnp.minimum(b, nu[0] - 1)

    def ftile(b, f, nu):
        return jnp.where(b < nu[0], f, n_f - 1)

    grid_spec = pltpu.PrefetchScalarGridSpec(
        num_scalar_prefetch=3,
        grid=(n_blocks, n_f),
        in_specs=[pl.BlockSpec((bm,), lambda b, f, be, nu, bv: (b,), memory_space=pltpu.SMEM),
                  pl.BlockSpec((bm,), lambda b, f, be, nu, bv: (jnp.minimum(b + 1, n_blocks - 1),),
                               memory_space=pltpu.SMEM),
                  pl.BlockSpec(memory_space=pl.ANY),
                  pl.BlockSpec(memory_space=pl.ANY),
                  pl.BlockSpec(memory_space=pl.ANY),
                  pl.BlockSpec(memory_space=pl.ANY),
                  pl.BlockSpec((None, 1, tf), lambda b, f, be, nu, bv: (be[blk(b, nu)], 0, ftile(b, f, nu))),
                  pl.BlockSpec((None, 1, tf), lambda b, f, be, nu, bv: (be[blk(b, nu)], 0, ftile(b, f, nu))),
                  pl.BlockSpec((None, 1, D_MODEL), lambda b, f, be, nu, bv: (be[blk(b, nu)], 0, 0))],
        out_specs=pl.BlockSpec((bm, D_MODEL), lambda b, f, be, nu, bv: (b, 0)),
        scratch_shapes=[pltpu.VMEM((2, bm, D_MODEL // 2), jnp.uint32),
                        pltpu.VMEM((bm, D_MODEL), bf16),
                        pltpu.VMEM((MOE_WBUF, D_MODEL, tf), f32),
                        pltpu.VMEM((MOE_WBUF, D_MODEL, tf), f32),
                        pltpu.VMEM((MOE_WBUF, tf, D_MODEL), f32),
                        pltpu.SemaphoreType.DMA((2,)),
                        pltpu.SemaphoreType.DMA((MOE_WBUF,))],
    )
    return pl.pallas_call(
        functools.partial(_expert_kernel, n_blocks=n_blocks),
        grid_spec=grid_spec,
        out_shape=jax.ShapeDtypeStruct((n_rows, D_MODEL), f32),
        compiler_params=_cparams(("arbitrary", "arbitrary")),
        name="moe_experts",
    )(block_expert, n_used, block_valid, row_tok, row_tok, hn_words, wg, wu, wd, bg, bu, bd)


def _combine_kernel(dest_ref, y_hbm, rg_ref, h1_ref, g_ref, o_ref, buf_ref, sem, *, tm):
    def row_copy(j, k):
        return pltpu.make_async_copy(y_hbm.at[pl.ds(dest_ref[j * TOP_K + k], 1)],
                                     buf_ref.at[k, pl.ds(j, 1)], sem)

    def issue(j, carry):
        for k in range(TOP_K):
            row_copy(j, k).start()
        return carry

    def drain(j, carry):
        for k in range(TOP_K):
            row_copy(j, k).wait()
        return carry

    lax.fori_loop(0, tm, issue, 0, unroll=8)
    lax.fori_loop(0, tm, drain, 0, unroll=8)
    rg = rg_ref[...]
    h = h1_ref[...]
    for k in range(TOP_K):
        h = h + buf_ref[k] * rg[:, k:k + 1]
    o_ref[...] = h * lax.rsqrt(jnp.mean(h * h, axis=-1, keepdims=True) + EPS) * g_ref[...]


def _combine(dest_flat, y, rg, h1, g_fin, tm):
    n = h1.shape[0]
    return pl.pallas_call(
        functools.partial(_combine_kernel, tm=tm),
        grid=(n // tm,),
        in_specs=[pl.BlockSpec((tm * TOP_K,), lambda i: (i,), memory_space=pltpu.SMEM),
                  pl.BlockSpec(memory_space=pl.ANY),
                  pl.BlockSpec((tm, LANES), lambda i: (i, 0)),
                  pl.BlockSpec((tm, D_MODEL), lambda i: (i, 0)),
                  pl.BlockSpec((1, D_MODEL), lambda i: (0, 0))],
        out_specs=pl.BlockSpec((tm, D_MODEL), lambda i: (i, 0)),
        out_shape=jax.ShapeDtypeStruct((n, D_MODEL), f32),
        scratch_shapes=[pltpu.VMEM((TOP_K, tm, D_MODEL), f32), pltpu.SemaphoreType.DMA],
        compiler_params=_cparams(("arbitrary",)),
        name="moe_combine_norm",
    )(dest_flat, y, rg, h1, g_fin)


def _rope_tables(n_pos):
    freqs = ROPE_THETA ** (-jnp.arange(0, ROT_DIM, 2, dtype=f32) / ROT_DIM)
    ang = jnp.arange(n_pos, dtype=f32)[:, None] * freqs[None, :]
    cos, sin = jnp.cos(ang), jnp.sin(ang)
    ones = jnp.ones((n_pos, HEAD_DIM - ROT_DIM), f32)
    zeros8 = jnp.zeros((n_pos, ROT_HALF), f32)
    zeros48 = jnp.zeros((n_pos, HEAD_DIM - ROT_DIM), f32)
    c = jnp.concatenate([cos, cos, ones], axis=1)
    s1 = jnp.concatenate([zeros8, sin, zeros48], axis=1)
    s2 = jnp.concatenate([-sin, zeros8, zeros48], axis=1)
    tile2 = lambda t: jnp.concatenate([t, t], axis=1)
    return (tile2(c), tile2(s1), tile2(s2)), (cos.T, sin.T)


def kernel(x, meta_tokens, norm_mix_g, w_in, pool_w, pool_scale, w_branch_pool, w_branch_attn, w_out,
           norm_moe_g, router_w, router_b, exp_w_gate, exp_b_gate, exp_w_up, exp_b_up, exp_w_down,
           exp_b_down, norm_final_g):
    batch, seq, _ = x.shape
    n_tok = batch * seq
    n_sel = min(TOPK_KEYS_MAX, (seq + N_META) // 4)
    x2d = x.reshape(n_tok, D_MODEL)
    kc, mc = ATTN_KC, ATTN_MC

    w = w_in[0]
    o_u, o_q, o_k, o_v, o_iq = 0, POOL_WIDTH, POOL_WIDTH + ATTN_WIDTH, POOL_WIDTH + 2 * ATTN_WIDTH, POOL_WIDTH + 3 * ATTN_WIDTH
    o_ik = o_iq + N_IDX_HEADS * IDX_DIM
    o_iw = o_ik + IDX_DIM
    o_gp = o_iw + N_IDX_HEADS
    o_ga = o_gp + D_MODEL
    cast = lambda a: a.astype(bf16)
    w_u, w_q, w_v, w_iq = (cast(w[:, o:o + 1024]) for o in (o_u, o_q, o_v, o_iq))
    w_kT = cast(w[:, o_k:o_k + ATTN_WIDTH].T)
    w_ikT = cast(w[:, o_ik:o_ik + IDX_DIM].T)
    w_iw = cast(jnp.pad(w[:, o_iw:o_iw + N_IDX_HEADS], ((0, 0), (0, LANES - N_IDX_HEADS)))
                * (N_IDX_HEADS ** -0.5 * IDX_DIM ** -0.5))
    w_gates = cast(w[:, o_gp:o_ga + D_MODEL])

    (tab_tok, (cosT, sinT)) = _rope_tables(seq + N_META)
    tok_x = tuple(t[N_META:] for t in tab_tok)
    cosT_x, sinT_x = cosT[:, N_META:], sinT[:, N_META:]
    pad_cols = lambda t, n: jnp.pad(t[:, :n] if t.shape[1] >= n else t, ((0, 0), (0, max(0, n - t.shape[1]))))
    cosT_m, sinT_m = pad_cols(cosT, mc), pad_cols(sinT, mc)

    xn = _rmsnorm(x2d, norm_mix_g[0], NORM_TM)
    meta_pad = jnp.pad(meta_tokens.astype(f32), ((0, mc - N_META), (0, 0)))
    xn_meta = _rmsnorm(meta_pad, norm_mix_g[0], mc)

    u = _proj(xn, w_u, PROJ_TM, f32)
    v = _proj(xn, w_v, PROJ_TM, bf16)
    sg = _proj(xn, w_gates, PROJ_TM, bf16, act="sigmoid")
    iw = _proj(xn, w_iw, PROJ_TM, f32)
    q = _proj_rope(xn, w_q, tok_x, PROJ_TM, seq // PROJ_TM, HEAD_DIM ** -0.5 * LOG2_E)
    iq = _proj_rope(xn, w_iq, tok_x, PROJ_TM, seq // PROJ_TM, 1.0)
    kT = _projT_rope(xn, w_kT, cosT_x, sinT_x, kc, seq // kc, N_HEADS)
    ikT = _projT_rope(xn, w_ikT, cosT_x, sinT_x, kc, seq // kc, 1)

    u_meta = _proj(xn_meta, w_u, mc, f32)[:N_META]
    v_meta = _proj(xn_meta, w_v, mc, bf16)
    kT_meta = _projT_rope(xn_meta, w_kT, cosT_m, sinT_m, mc, 1, N_HEADS)[0]
    ikT_meta = _projT_rope(xn_meta, w_ikT, cosT_m, sinT_m, mc, 1, 1)[0]

    pp = _pool(u.reshape(batch, seq, POOL_WIDTH), u_meta, cast(pool_w[0]), pool_scale[0].reshape(1, POOL_WIDTH),
               POOL_TM).reshape(n_tok, POOL_WIDTH)
    at = _attention(q, iq, iw, kT.reshape(batch, seq // kc, ATTN_WIDTH, kc), v,
                    ikT.reshape(batch, seq // kc, IDX_DIM, kc), kT_meta, v_meta, ikT_meta, batch, seq, n_sel)

    rw_f = router_w[0].astype(f32)
    rw_hi = rw_f.astype(bf16)
    rw_lo = (rw_f - rw_hi.astype(f32)).astype(bf16)
    rw = jnp.pad(jnp.concatenate([rw_hi, rw_lo], axis=1), ((0, 0), (0, LANES - 2 * N_EXPERTS)))
    rb = jnp.pad(router_b[0].astype(f32), (0, LANES - N_EXPERTS)).reshape(1, LANES)
    h1, hn, logits = _post(x2d, pp, at, sg, sg, cast(w_branch_pool[0]), cast(w_branch_attn[0]), cast(w_out[0]),
                           norm_moe_g[0].reshape(1, D_MODEL), rw, rb, POST_TM)
    ri, rg, cnt = _router(logits, ROUTER_TM)

    bm = MOE_BM
    counts = cnt[0, :N_EXPERTS].astype(i32)
    padded = ((counts + bm - 1) // bm) * bm
    pad_end = jnp.cumsum(padded)
    pad_start = pad_end - padded
    n_blocks = -(-(n_tok * TOP_K + N_EXPERTS * (bm - 1)) // bm)
    expert_ids = jnp.arange(N_EXPERTS, dtype=i32)[None, None, :]
    seg_start = jnp.sum(jnp.where(ri[:, :TOP_K, None] == expert_ids, pad_start[None, None, :], 0), axis=-1)
    dest = (seg_start + ri[:, TOP_K:2 * TOP_K]).reshape(n_tok * TOP_K)
    block_rows = jnp.arange(n_blocks, dtype=i32) * bm
    block_expert = jnp.minimum(jnp.sum((pad_end[None, :] <= block_rows[:, None]).astype(i32), axis=1),
                               N_EXPERTS - 1)
    n_used = (pad_end[-1] // bm).astype(i32).reshape(1)
    block_valid = jnp.clip(pad_start[block_expert] + counts[block_expert] - block_rows, 0, bm).astype(i32)

    row_tok = jnp.zeros((n_blocks * bm,), i32).at[dest].set(jnp.arange(n_tok * TOP_K, dtype=i32) // TOP_K)
    y = _experts(block_expert, n_used, block_valid, row_tok, hn, exp_w_gate[0], exp_w_up[0], exp_w_down[0],
                 exp_b_gate[0].reshape(N_EXPERTS, 1, D_FF), exp_b_up[0].reshape(N_EXPERTS, 1, D_FF),
                 exp_b_down[0].reshape(N_EXPERTS, 1, D_MODEL), bm, MOE_TF)
    out = _combine(dest, y, rg, h1, norm_final_g.reshape(1, D_MODEL), COMBINE_TM)
    return out.reshape(batch, seq, D_MODEL)
```
